```python
import math
import jax, jax.numpy as jnp
from jax import lax
import numpy as np

D_MODEL = 1024
BATCH = 16
SEQ = 4096
DEPTH = 2
DEC_BATCH = 32
DEC_SEQ = 2048
PAST_LEN = 128

CONV_WIDTH = D_MODEL // 2
ATTN_WIDTH = D_MODEL - CONV_WIDTH
N_ATTN_HEADS = 4
ATTN_VDIM = ATTN_WIDTH // N_ATTN_HEADS
ATTN_QKDIM = ATTN_VDIM // 2
CONV_K = 3
Q_BLOCK = 128
ROPE_THETA = 10000.0
SPLITS = (CONV_WIDTH, 2 * CONV_WIDTH, 3 * CONV_WIDTH,
          3 * CONV_WIDTH + ATTN_WIDTH, 3 * CONV_WIDTH + 2 * ATTN_WIDTH)
IN_COLS = 3 * CONV_WIDTH + 3 * ATTN_WIDTH
PEER_HEADS = 8
PEER_NKEYS = 128
PEER_N = PEER_NKEYS * PEER_NKEYS
PEER_QDIM = 256
PEER_HALF = PEER_QDIM // 2
PEER_TOPK = 16
TOKEN_BLOCK = 128
LN_EPS = 1e-5
ALPHA = (2 * DEPTH) ** 0.25
BETA = (8 * DEPTH) ** -0.25

kernel_name = "hybrid_conv_diffattn_peer_encoder"


def layer_norm(x, g, b):
    xf = x.astype(jnp.float32)
    mu = jnp.mean(xf, axis=-1, keepdims=True)
    var = jnp.mean(jnp.square(xf - mu), axis=-1, keepdims=True)
    return ((xf - mu) * lax.rsqrt(var + LN_EPS) * g.astype(jnp.float32) + b.astype(jnp.float32)).astype(x.dtype)


def rope_tables(s, dim, dtype):
    inv_freq = ROPE_THETA ** (-jnp.arange(0, dim, 2, dtype=jnp.float32) / dim)
    ang = jnp.arange(s, dtype=jnp.float32)[:, None] * inv_freq[None, :]
    ang = jnp.concatenate([ang, ang], axis=-1)
    return jnp.cos(ang).astype(dtype)[:, None, :], jnp.sin(ang).astype(dtype)[:, None, :]


def apply_rope(x, cos, sin):
    h = x.shape[-1] // 2
    rot = jnp.concatenate([-x[..., h:], x[..., :h]], axis=-1)
    return x * cos + rot * sin


def mixer(x, w_in, w_out, conv_w, lq1, lk1, lq2, lk2, subln_g, lam_init):
    b, s, _ = x.shape
    proj = x @ w_in
    g_b, g_c, h, q, k, v = jnp.split(proj, SPLITS, axis=-1)
    u = g_c * h
    up = jnp.pad(u, ((0, 0), (1, 1), (0, 0)))
    conv = up[:, :-2] * conv_w[0] + up[:, 1:-1] * conv_w[1] + up[:, 2:] * conv_w[2]
    y_conv = g_b * conv
    cos, sin = rope_tables(s, ATTN_QKDIM, x.dtype)
    q = apply_rope(q.reshape(b, s, 2 * N_ATTN_HEADS, ATTN_QKDIM), cos, sin)
    k = apply_rope(k.reshape(b, s, 2 * N_ATTN_HEADS, ATTN_QKDIM), cos, sin)
    nb = s // Q_BLOCK
    qb = q.reshape(b, nb, Q_BLOCK, N_ATTN_HEADS, 2, ATTN_QKDIM).transpose(1, 0, 3, 4, 2, 5)
    kt = k.reshape(b, s, N_ATTN_HEADS, 2, ATTN_QKDIM).transpose(0, 2, 3, 1, 4)
    vt = v.reshape(b, s, N_ATTN_HEADS, ATTN_VDIM).transpose(0, 2, 1, 3)
    lam = (jnp.exp(jnp.sum(lq1.astype(jnp.float32) * lk1.astype(jnp.float32)))
           - jnp.exp(jnp.sum(lq2.astype(jnp.float32) * lk2.astype(jnp.float32))) + lam_init)
    scale = 1.0 / math.sqrt(ATTN_QKDIM)

    def attend(qblk):
        sc = jnp.einsum('bhmqd,bhmkd->bhmqk', qblk, kt).astype(jnp.float32) * scale
        p = jax.nn.softmax(sc, axis=-1)
        w = p[:, :, 0] - lam * p[:, :, 1]
        return jnp.einsum('bhqk,bhkd->bhqd', w.astype(vt.dtype), vt)

    o = lax.map(attend, qb)
    o = o.transpose(1, 0, 3, 2, 4).reshape(b, s, N_ATTN_HEADS, ATTN_VDIM)
    of = o.astype(jnp.float32)
    of = of * lax.rsqrt(jnp.mean(of * of, axis=-1, keepdims=True) + LN_EPS) * subln_g.astype(jnp.float32)
    y_attn = (of * (1.0 - lam_init)).astype(x.dtype).reshape(b, s, ATTN_WIDTH)
    return jnp.concatenate([y_conv, y_attn], axis=-1) @ w_out


def peer(x, wq, sub_keys, u_tab, v_tab):
    b, s, d = x.shape
    xt = x.reshape(-1, TOKEN_BLOCK, d)

    def retrieve(xb):
        q = (xb @ wq).reshape(TOKEN_BLOCK, PEER_HEADS, 2, PEER_HALF)
        sc = jnp.einsum('thpc,pnc->thpn', q, sub_keys).astype(jnp.float32)
        s1, i1 = lax.top_k(sc[:, :, 0], PEER_TOPK)
        s2, i2 = lax.top_k(sc[:, :, 1], PEER_TOPK)
        comb = (s1[..., :, None] + s2[..., None, :]).reshape(TOKEN_BLOCK, PEER_HEADS, PEER_TOPK * PEER_TOPK)
        cidx = (i1[..., :, None] * PEER_NKEYS + i2[..., None, :]).reshape(TOKEN_BLOCK, PEER_HEADS, PEER_TOPK * PEER_TOPK)
        top_s, top_j = lax.top_k(comb, PEER_TOPK)
        eidx = jnp.take_along_axis(cidx, top_j, axis=-1)
        g = jax.nn.softmax(top_s, axis=-1)
        u = jnp.take(u_tab, eidx, axis=0)
        hval = jnp.einsum('thkd,td->thk', u, xb).astype(jnp.float32)
        a = (g * jax.nn.gelu(hval, approximate=False)).astype(xb.dtype)
        vv = jnp.take(v_tab, eidx, axis=0)
        return jnp.einsum('thk,thkd->td', a, vv)

    return lax.map(retrieve, xt).reshape(b, s, d)


def trunk(x, w_in, w_out, conv_w, lam_q1, lam_k1, lam_q2, lam_k2, subln_g, ln1_g, ln1_b,
          peer_wq, peer_keys, peer_u, peer_v, ln2_g, ln2_b):
    for l in range(DEPTH):
        lam_init = 0.8 - 0.6 * math.exp(-0.3 * l)
        m = mixer(x, w_in[l], w_out[l], conv_w[l], lam_q1[l], lam_k1[l], lam_q2[l], lam_k2[l], subln_g[l], lam_init)
        x = layer_norm(ALPHA * x + m, ln1_g[l], ln1_b[l])
        f = peer(x, peer_wq[l], peer_keys[l], peer_u[l], peer_v[l])
        x = layer_norm(ALPHA * x + f, ln2_g[l], ln2_b[l])
    return x


def setup_inputs(seed: int = 0) -> dict:
    key = jax.random.key(seed)
    ks = jax.random.split(key, 20)
    f32 = jnp.float32
    nrm = lambda k, shp: jax.random.normal(k, shp, dtype=f32)
    col_scale = jnp.concatenate([
        jnp.ones((2 * CONV_WIDTH,), f32), jnp.full((CONV_WIDTH,), BETA, f32),
        jnp.ones((2 * ATTN_WIDTH,), f32), jnp.full((ATTN_WIDTH,), BETA, f32)])
    return {
        "x_prompt": nrm(ks[0], (BATCH, SEQ, D_MODEL)),
        "x_sample": nrm(ks[1], (DEC_BATCH, DEC_SEQ, D_MODEL)),
        "w_in": nrm(ks[2], (DEPTH, D_MODEL, IN_COLS)) * (D_MODEL ** -0.5) * col_scale,
        "w_out": nrm(ks[3], (DEPTH, D_MODEL, D_MODEL)) * (D_MODEL ** -0.5) * BETA,
        "conv_w": nrm(ks[4], (DEPTH, CONV_K, CONV_WIDTH)) * 0.5,
        "lam_q1": nrm(ks[5], (DEPTH, ATTN_QKDIM)) * 0.1,
        "lam_k1": nrm(ks[6], (DEPTH, ATTN_QKDIM)) * 0.1,
        "lam_q2": nrm(ks[7], (DEPTH, ATTN_QKDIM)) * 0.1,
        "lam_k2": nrm(ks[8], (DEPTH, ATTN_QKDIM)) * 0.1,
        "subln_g": 1.0 + 0.01 * nrm(ks[9], (DEPTH, ATTN_VDIM)),
        "ln1_g": 1.0 + 0.01 * nrm(ks[10], (DEPTH, D_MODEL)),
        "ln1_b": 0.01 * nrm(ks[11], (DEPTH, D_MODEL)),
        "peer_wq": nrm(ks[12], (DEPTH, D_MODEL, PEER_HEADS * PEER_QDIM)) * (D_MODEL ** -0.5),
        "peer_keys": nrm(ks[13], (DEPTH, 2, PEER_NKEYS, PEER_HALF)) * (PEER_HALF ** -0.5),
        "peer_u": nrm(ks[14], (DEPTH, PEER_N, D_MODEL)) * (D_MODEL ** -0.5),
        "peer_v": nrm(ks[15], (DEPTH, PEER_N, D_MODEL)) * (BETA * 0.5),
        "ln2_g": 1.0 + 0.01 * nrm(ks[16], (DEPTH, D_MODEL)),
        "ln2_b": 0.01 * nrm(ks[17], (DEPTH, D_MODEL)),
    }


def reference(x_prompt, x_sample, w_in, w_out, conv_w, lam_q1, lam_k1, lam_q2, lam_k2, subln_g,
              ln1_g, ln1_b, peer_wq, peer_keys, peer_u, peer_v, ln2_g, ln2_b):
    y_prompt = trunk(x_prompt, w_in, w_out, conv_w, lam_q1, lam_k1, lam_q2, lam_k2, subln_g, ln1_g, ln1_b,
                     peer_wq, peer_keys, peer_u, peer_v, ln2_g, ln2_b)
    y_sample = trunk(x_sample, w_in, w_out, conv_w, lam_q1, lam_k1, lam_q2, lam_k2, subln_g, ln1_g, ln1_b,
                     peer_wq, peer_keys, peer_u, peer_v, ln2_g, ln2_b)
    return (y_prompt, y_sample)
```

```python
import functools
import math

import jax
import jax.numpy as jnp
from jax import lax
from jax.experimental import pallas as pl
from jax.experimental.pallas import tpu as pltpu

F32 = jnp.float32
BF16 = jnp.bfloat16

D_MODEL = 1024
CONV_WIDTH = 512
ATTN_WIDTH = 512
N_ATTN_HEADS = 4
ATTN_VDIM = 128
ATTN_QKDIM = 64
ROPE_HALF = ATTN_QKDIM // 2
IN_COLS = 3 * CONV_WIDTH + 3 * ATTN_WIDTH
ROPE_THETA = 10000.0
PEER_HEADS = 8
PEER_NKEYS = 128
PEER_N = PEER_NKEYS * PEER_NKEYS
PEER_HALF = 128
PEER_QCOLS = PEER_HEADS * 2 * PEER_HALF
PEER_TOPK = 16
PEER_SEL = PEER_HEADS * PEER_TOPK
LN_EPS = 1e-5

V7X_LANES = 128
V7X_SUBLANES = 8
V7X_VMEM_BYTES = 64 * 1024 * 1024
MIB = 1024 * 1024

INPROJ_ROWS = 512
ATTN_Q_ROWS = 256
OUTPROJ_ROWS = 512
ROUTE_ROWS = 256
PEER_ROWS = 256
PEER_EXPERTS = 512
PEER_GROUPS = PEER_EXPERTS // PEER_NKEYS
GATE_PITCH = PEER_NKEYS + V7X_SUBLANES

NT_DIMS = (((1,), (1,)), ((), ()))


def _vmem_limit(nbytes):
    return int(min(V7X_VMEM_BYTES - 6 * MIB, max(32 * MIB, nbytes)))


def _layer_norm(z, g, b):
    mu = jnp.mean(z, axis=-1, keepdims=True)
    zc = z - mu
    var = jnp.mean(zc * zc, axis=-1, keepdims=True)
    return zc * lax.rsqrt(var + LN_EPS) * g + b


def _inproj_kernel(x_ref, xp_ref, xn_ref, w_ref, cw_ref, cos_ref, sin_ref,
                   yc_ref, q_ref, k_ref, v_ref, *, blocks_per_seq):
    tm = x_ref.shape[0]
    pos = lax.rem(pl.program_id(0), blocks_per_seq)
    xe = jnp.concatenate([x_ref[...], xp_ref[...], xn_ref[...]], axis=0).astype(BF16)
    xb = xe[:tm]

    def proj(v, c0, c1):
        return jnp.dot(v, w_ref[:, c0:c1], preferred_element_type=F32)

    g_b = proj(xb, 0, CONV_WIDTH)
    ue = proj(xe, CONV_WIDTH, 2 * CONV_WIDTH) * proj(xe, 2 * CONV_WIDTH, 3 * CONV_WIDTH)
    u = ue[:tm]
    u_prev = jnp.where(pos == 0, 0.0, ue[tm + V7X_SUBLANES - 1:tm + V7X_SUBLANES])
    u_next = jnp.where(pos == blocks_per_seq - 1, 0.0, ue[tm + V7X_SUBLANES:tm + V7X_SUBLANES + 1])
    row = lax.broadcasted_iota(jnp.int32, u.shape, 0)
    u_m1 = jnp.where(row == 0, u_prev, pltpu.roll(u, 1, 0))
    u_p1 = jnp.where(row == tm - 1, u_next, pltpu.roll(u, tm - 1, 0))
    conv = u_m1 * cw_ref[0:1, :] + u * cw_ref[1:2, :] + u_p1 * cw_ref[2:3, :]
    yc_ref[...] = (g_b * conv).astype(yc_ref.dtype)

    cos = cos_ref[...]
    sin = sin_ref[...]
    lane = lax.broadcasted_iota(jnp.int32, cos.shape, 1)
    first_half = jnp.bitwise_and(lane, ATTN_QKDIM - 1) < ROPE_HALF

    def rope(t):
        n = t.shape[1]
        rot = jnp.where(first_half, pltpu.roll(t, n - ROPE_HALF, 1), pltpu.roll(t, ROPE_HALF, 1))
        return t * cos + rot * sin

    c = 3 * CONV_WIDTH
    scale = 1.0 / math.sqrt(ATTN_QKDIM)
    q_ref[...] = (rope(proj(xb, c, c + ATTN_WIDTH)) * scale).astype(q_ref.dtype)
    k_ref[...] = rope(proj(xb, c + ATTN_WIDTH, c + 2 * ATTN_WIDTH)).astype(k_ref.dtype)
    v_ref[...] = proj(xb, c + 2 * ATTN_WIDTH, c + 3 * ATTN_WIDTH).astype(v_ref.dtype)


def _inproj(x2d, w_in, conv_w, cos, sin, seq):
    n = x2d.shape[0]
    tm = min(INPROJ_ROWS, seq)
    bps = seq // tm
    nblk = n // tm
    r8 = tm // V7X_SUBLANES
    last8 = n // V7X_SUBLANES - 1
    row_spec = lambda w: pl.BlockSpec((tm, w), lambda i: (i, 0))
    tab_spec = pl.BlockSpec((tm, ATTN_WIDTH), lambda i: (lax.rem(i, bps), 0))
    out = jax.ShapeDtypeStruct((n, CONV_WIDTH), BF16)
    est = 2 * (tm * D_MODEL * 4 + D_MODEL * IN_COLS * 2 + 2 * tm * ATTN_WIDTH * 4 + 4 * tm * 512 * 2) \
        + 12 * tm * 512 * 4 + 2 * tm * D_MODEL * 4
    return pl.pallas_call(
        functools.partial(_inproj_kernel, blocks_per_seq=bps),
        grid=(nblk,),
        in_specs=[
            row_spec(D_MODEL),
            pl.BlockSpec((V7X_SUBLANES, D_MODEL), lambda i: (jnp.maximum(i * r8 - 1, 0), 0)),
            pl.BlockSpec((V7X_SUBLANES, D_MODEL), lambda i: (jnp.minimum((i + 1) * r8, last8), 0)),
            pl.BlockSpec((D_MODEL, IN_COLS), lambda i: (0, 0)),
            pl.BlockSpec((3, CONV_WIDTH), lambda i: (0, 0)),
            tab_spec, tab_spec,
        ],
        out_specs=[row_spec(CONV_WIDTH)] * 4,
        out_shape=[out] * 4,
        compiler_params=pltpu.CompilerParams(
            dimension_semantics=("parallel",), vmem_limit_bytes=_vmem_limit(est)),
        name="inproj_conv_rope",
    )(x2d, x2d, x2d, w_in, conv_w, cos, sin)


def _attn_kernel(q_ref, k_ref, v_ref, lq1_ref, lk1_ref, lq2_ref, lk2_ref, g_ref, o_ref, *, lam_init):
    q = q_ref[...]
    k = k_ref[...]
    v = v_ref[...]
    lane = lax.broadcasted_iota(jnp.int32, q.shape, 1)
    zero = jnp.zeros_like(q)

    def softmax_pv(qm):
        sc = lax.dot_general(qm, k, NT_DIMS, preferred_element_type=F32)
        m = jnp.max(sc, axis=-1, keepdims=True)
        e = jnp.exp(sc - m)
        l = jnp.sum(e, axis=-1, keepdims=True)
        return jnp.dot(e.astype(v.dtype), v, preferred_element_type=F32) / l

    lam = (jnp.exp(jnp.sum(lq1_ref[...] * lk1_ref[...], axis=-1, keepdims=True))
           - jnp.exp(jnp.sum(lq2_ref[...] * lk2_ref[...], axis=-1, keepdims=True)) + lam_init)
    o = softmax_pv(jnp.where(lane < ATTN_QKDIM, q, zero)) - lam * softmax_pv(jnp.where(lane >= ATTN_QKDIM, q, zero))
    o = o * lax.rsqrt(jnp.mean(o * o, axis=-1, keepdims=True) + LN_EPS) * g_ref[...]
    o_ref[...] = (o * (1.0 - lam_init)).astype(o_ref.dtype)


def _attention(q, k, v, lq1, lk1, lq2, lk2, subln_g, lam_init):
    b, s, _ = q.shape
    tq = min(ATTN_Q_ROWS, s)
    q_spec = pl.BlockSpec((None, tq, ATTN_VDIM), lambda bi, h, i: (bi, i, h))
    kv_spec = pl.BlockSpec((None, s, ATTN_VDIM), lambda bi, h, i: (bi, 0, h))
    vec = lambda w: pl.BlockSpec((1, w), lambda bi, h, i: (0, 0))
    est = 4 * s * ATTN_VDIM * 2 + 2 * tq * s * (4 + 4 + 2) + 8 * tq * ATTN_VDIM * 4
    return pl.pallas_call(
        functools.partial(_attn_kernel, lam_init=lam_init),
        grid=(b, N_ATTN_HEADS, s // tq),
        in_specs=[q_spec, kv_spec, kv_spec, vec(ATTN_QKDIM), vec(ATTN_QKDIM), vec(ATTN_QKDIM),
                  vec(ATTN_QKDIM), vec(ATTN_VDIM)],
        out_specs=q_spec,
        out_shape=jax.ShapeDtypeStruct((b, s, ATTN_WIDTH), BF16),
        compiler_params=pltpu.CompilerParams(
            dimension_semantics=("parallel", "parallel", "parallel"), vmem_limit_bytes=_vmem_limit(est)),
        name="diff_attention",
    )(q, k, v, lq1, lk1, lq2, lk2, subln_g)


def _outproj_kernel(yc_ref, ya_ref, x_ref, w_ref, g_ref, b_ref, o_ref, *, alpha):
    m = jnp.dot(yc_ref[...], w_ref[:CONV_WIDTH, :], preferred_element_type=F32)
    m = m + jnp.dot(ya_ref[...], w_ref[CONV_WIDTH:, :], preferred_element_type=F32)
    o_ref[...] = _layer_norm(alpha * x_ref[...] + m, g_ref[...], b_ref[...])


def _outproj(yc, ya, x2d, w_out, g, b, alpha):
    n = x2d.shape[0]
    tm = min(OUTPROJ_ROWS, n)
    row_spec = lambda w: pl.BlockSpec((tm, w), lambda i: (i, 0))
    vec = pl.BlockSpec((1, D_MODEL), lambda i: (0, 0))
    est = 2 * (2 * tm * 512 * 2 + 2 * tm * D_MODEL * 4 + D_MODEL * D_MODEL * 2) + 4 * tm * D_MODEL * 4
    return pl.pallas_call(
        functools.partial(_outproj_kernel, alpha=alpha),
        grid=(n // tm,),
        in_specs=[row_spec(CONV_WIDTH), row_spec(ATTN_WIDTH), row_spec(D_MODEL),
                  pl.BlockSpec((D_MODEL, D_MODEL), lambda i: (0, 0)), vec, vec],
        out_specs=row_spec(D_MODEL),
        out_shape=jax.ShapeDtypeStruct((n, D_MODEL), F32),
        compiler_params=pltpu.CompilerParams(
            dimension_semantics=("parallel",), vmem_limit_bytes=_vmem_limit(est)),
        name="outproj_ln",
    )(yc, ya, x2d, w_out, g, b)


def _top16(s, store_val, store_pos):
    n = s.shape[0]
    rows = lax.broadcasted_iota(jnp.int32, s.shape, 0).astype(F32)
    for r in range(PEER_TOPK):
        m = jnp.max(s, axis=0, keepdims=True)
        pos = jnp.min(jnp.where(s == m, rows, float(n)), axis=0, keepdims=True)
        s = jnp.where(rows == pos, -jnp.inf, s)
        store_val(r, m)
        store_pos(r, pos)


def _route_kernel(x_ref, wq_ref, keys_ref, i1_ref, i2_ref, g_ref,
                  sc_ref, tv_ref, tp_ref, sel1_ref, sel2_ref, gate_ref):
    tm = x_ref.shape[0]
    q = jnp.dot(x_ref[...].astype(BF16), wq_ref[...], preferred_element_type=F32)
    for hp in range(2 * PEER_HEADS):
        qhp = q[:, hp * PEER_HALF:(hp + 1) * PEER_HALF].astype(BF16)
        sc_ref[hp] = lax.dot_general(keys_ref[hp % 2], qhp, NT_DIMS, preferred_element_type=F32)

    def store(ref, slot):
        def f(r, val):
            ref[slot, r:r + 1, :] = val
        return f

    def head_body(h, carry):
        for c in range(tm // V7X_LANES):
            lanes = slice(c * V7X_LANES, (c + 1) * V7X_LANES)
            for p in range(2):
                _top16(sc_ref[2 * h + p, :, lanes], store(tv_ref, p), store(tp_ref, p))
            s1 = tv_ref[0]
            s2 = tv_ref[1]
            comb = jnp.concatenate([s1[a:a + 1, :] + s2 for a in range(PEER_TOPK)], axis=0)
            _top16(comb, store(tv_ref, 2), store(tp_ref, 2))
            top_s = tv_ref[2]
            pos = tp_ref[2].astype(jnp.int32)
            a_sel = jnp.right_shift(pos, 4)
            b_sel = jnp.bitwise_and(pos, PEER_TOPK - 1)
            i1 = tp_ref[0]
            i2 = tp_ref[1]
            e1 = jnp.zeros_like(top_s)
            e2 = jnp.zeros_like(top_s)
            for a in range(PEER_TOPK):
                e1 = jnp.where(a_sel == a, i1[a:a + 1, :], e1)
                e2 = jnp.where(b_sel == a, i2[a:a + 1, :], e2)
            ex = jnp.exp(top_s - top_s[0:1, :])
            gate = ex / jnp.sum(ex, axis=0, keepdims=True)
            base = pl.multiple_of(h * PEER_TOPK, PEER_TOPK)
            sel1_ref[pl.ds(base, PEER_TOPK), lanes] = e1
            sel2_ref[pl.ds(base, PEER_TOPK), lanes] = e2
            gate_ref[pl.ds(base, PEER_TOPK), lanes] = gate
        return carry

    lax.fori_loop(0, PEER_HEADS, head_body, 0)
    for c in range(tm // V7X_LANES):
        lanes = slice(c * V7X_LANES, (c + 1) * V7X_LANES)
        rows = slice(c * V7X_LANES, (c + 1) * V7X_LANES)
        i1_ref[rows, :] = sel1_ref[:, lanes].T
        i2_ref[rows, :] = sel2_ref[:, lanes].T
        g_ref[rows, :] = gate_ref[:, lanes].T


def _route(x2d, wq, keys):
    n = x2d.shape[0]
    tm = min(ROUTE_ROWS, n)
    out = jax.ShapeDtypeStruct((n, PEER_SEL), F32)
    out_spec = pl.BlockSpec((tm, PEER_SEL), lambda i: (i, 0))
    est = 2 * (tm * D_MODEL * 4 + D_MODEL * PEER_QCOLS * 2) + 3 * tm * PEER_QCOLS * 4 + 8 * tm * PEER_SEL * 4
    return pl.pallas_call(
        _route_kernel,
        grid=(n // tm,),
        in_specs=[pl.BlockSpec((tm, D_MODEL), lambda i: (i, 0)),
                  pl.BlockSpec((D_MODEL, PEER_QCOLS), lambda i: (0, 0)),
                  pl.BlockSpec((2, PEER_NKEYS, PEER_HALF), lambda i: (0, 0, 0))],
        out_specs=[out_spec] * 3,
        out_shape=[out] * 3,
        scratch_shapes=[pltpu.VMEM((2 * PEER_HEADS, PEER_NKEYS, tm), F32),
                        pltpu.VMEM((3, PEER_TOPK, V7X_LANES), F32),
                        pltpu.VMEM((3, PEER_TOPK, V7X_LANES), F32),
                        pltpu.VMEM((PEER_SEL, tm), F32),
                        pltpu.VMEM((PEER_SEL, tm), F32),
                        pltpu.VMEM((PEER_SEL, tm), F32)],
        compiler_params=pltpu.CompilerParams(
            dimension_semantics=("parallel",), vmem_limit_bytes=_vmem_limit(est)),
        name="peer_route",
    )(x2d, wq, keys)


def _peer_kernel(x_ref, i1_ref, i2_ref, g_ref, ut_ref, v_ref, lng_ref, lnb_ref, o_ref,
                 xb_ref, gate_ref, acc_ref, *, alpha):
    tm = x_ref.shape[0]
    j = pl.program_id(1)

    @pl.when(j == 0)
    def _():
        xb_ref[...] = x_ref[...].astype(BF16)
        acc_ref[...] = jnp.zeros_like(acc_ref)
        key_row = lax.broadcasted_iota(jnp.int32, (PEER_NKEYS, PEER_SEL), 0).astype(F32)

        def token_body(t, carry):
            r1 = i1_ref[pl.ds(t, 1), :]
            r2 = i2_ref[pl.ds(t, 1), :]
            gg = g_ref[pl.ds(t, 1), :]
            hot1 = jnp.where(key_row == r1, 1.0, 0.0).astype(BF16)
            wgt2 = jnp.where(key_row == r2, gg, 0.0).astype(BF16)
            tile = lax.dot_general(hot1, wgt2, NT_DIMS, preferred_element_type=F32)
            gate_ref[pl.ds(pl.multiple_of(t * GATE_PITCH, V7X_SUBLANES), PEER_NKEYS), :] = tile
            return carry

        lax.fori_loop(0, tm, token_body, 0)

    h = jnp.dot(xb_ref[...], ut_ref[...], preferred_element_type=F32)
    parts = []
    for r in range(PEER_GROUPS):
        gate = gate_ref[pl.ds(j * PEER_GROUPS + r, tm, stride=GATE_PITCH), :]
        hr = h[:, r * PEER_NKEYS:(r + 1) * PEER_NKEYS]
        gelu = 0.5 * hr * (1.0 + lax.erf(hr * math.sqrt(0.5)))
        parts.append((gelu * gate).astype(BF16))
    a = jnp.concatenate(parts, axis=1)
    acc_ref[...] += jnp.dot(a, v_ref[...], preferred_element_type=F32)

    @pl.when(j == pl.num_programs(1) - 1)
    def _():
        o_ref[...] = _layer_norm(alpha * x_ref[...] + acc_ref[...], lng_ref[...], lnb_ref[...])


def _peer_mix(x2d, i1, i2, g, u_t, v_tab, ln_g, ln_b, alpha):
    n = x2d.shape[0]
    tm = min(PEER_ROWS, n)
    row_spec = lambda w: pl.BlockSpec((tm, w), lambda i, j: (i, 0))
    vec = pl.BlockSpec((1, D_MODEL), lambda i, j: (0, 0))
    gate_bytes = tm * GATE_PITCH * PEER_NKEYS * 4
    est = gate_bytes + 2 * (2 * tm * D_MODEL * 4 + 3 * tm * PEER_SEL * 4 + 2 * PEER_EXPERTS * D_MODEL * 2) \
        + tm * D_MODEL * 6 + 4 * tm * PEER_EXPERTS * 4
    return pl.pallas_call(
        functools.partial(_peer_kernel, alpha=alpha),
        grid=(n // tm, PEER_N // PEER_EXPERTS),
        in_specs=[row_spec(D_MODEL), row_spec(PEER_SEL), row_spec(PEER_SEL), row_spec(PEER_SEL),
                  pl.BlockSpec((D_MODEL, PEER_EXPERTS), lambda i, j: (0, j)),
                  pl.BlockSpec((PEER_EXPERTS, D_MODEL), lambda i, j: (j, 0)),
                  vec, vec],
        out_specs=row_spec(D_MODEL),
        out_shape=jax.ShapeDtypeStruct((n, D_MODEL), F32),
        scratch_shapes=[pltpu.VMEM((tm, D_MODEL), BF16),
                        pltpu.VMEM((tm * GATE_PITCH, PEER_NKEYS), F32),
                        pltpu.VMEM((tm, D_MODEL), F32)],
        compiler_params=pltpu.CompilerParams(
            dimension_semantics=("parallel", "arbitrary"), vmem_limit_bytes=_vmem_limit(est)),
        name="peer_mix_ln",
    )(x2d, i1, i2, g, u_t, v_tab, ln_g, ln_b)


def _rope_tables(seq):
    inv_freq = ROPE_THETA ** (-jnp.arange(0, ATTN_QKDIM, 2, dtype=F32) / ATTN_QKDIM)
    ang = jnp.arange(seq, dtype=F32)[:, None] * inv_freq[None, :]
    ang = jnp.concatenate([ang, ang], axis=-1)
    sign = jnp.where(jnp.arange(ATTN_QKDIM) < ROPE_HALF, -1.0, 1.0).astype(F32)
    reps = ATTN_WIDTH // ATTN_QKDIM
    return jnp.tile(jnp.cos(ang), (1, reps)), jnp.tile(jnp.sin(ang) * sign, (1, reps))


def _trunk(x, layers, depth):
    b, s, _ = x.shape
    alpha = (2 * depth) ** 0.25
    cos, sin = _rope_tables(s)
    x2d = x.reshape(b * s, D_MODEL)
    for l, p in enumerate(layers):
        lam_init = 0.8 - 0.6 * math.exp(-0.3 * l)
        yc, q, k, v = _inproj(x2d, p["w_in"], p["conv_w"], cos, sin, s)
        shp = (b, s, ATTN_WIDTH)
        ya = _attention(q.reshape(shp), k.reshape(shp), v.reshape(shp),
                        p["lq1"], p["lk1"], p["lq2"], p["lk2"], p["subln_g"], lam_init)
        x2d = _outproj(yc, ya.reshape(b * s, ATTN_WIDTH), x2d, p["w_out"], p["ln1_g"], p["ln1_b"], alpha)
        i1, i2, g = _route(x2d, p["wq"], p["keys"])
        x2d = _peer_mix(x2d, i1, i2, g, p["u_t"], p["v"], p["ln2_g"], p["ln2_b"], alpha)
    return x2d.reshape(b, s, D_MODEL)


def kernel(x_prompt, x_sample, w_in, w_out, conv_w, lam_q1, lam_k1, lam_q2, lam_k2, subln_g, ln1_g, ln1_b,
           peer_wq, peer_keys, peer_u, peer_v, ln2_g, ln2_b):
    depth = w_in.shape[0]
    layers = []
    for l in range(depth):
        row = lambda a: a[l][None, :]
        layers.append(dict(
            w_in=w_in[l].astype(BF16), w_out=w_out[l].astype(BF16), conv_w=conv_w[l],
            lq1=row(lam_q1), lk1=row(lam_k1), lq2=row(lam_q2), lk2=row(lam_k2), subln_g=row(subln_g),
            ln1_g=row(ln1_g), ln1_b=row(ln1_b), ln2_g=row(ln2_g), ln2_b=row(ln2_b),
            wq=peer_wq[l].astype(BF16), keys=peer_keys[l].astype(BF16),
            u_t=peer_u[l].astype(BF16).T, v=peer_v[l].astype(BF16)))
    return (_trunk(x_prompt, layers, depth), _trunk(x_sample, layers, depth))
```

```python
import functools
import math

import jax
import jax.numpy as jnp
from jax import lax
from jax.experimental import pallas as pl
from jax.experimental.pallas import tpu as pltpu

F32 = jnp.float32
BF16 = jnp.bfloat16

D_MODEL = 1024
CONV_WIDTH = 512
ATTN_WIDTH = 512
N_ATTN_HEADS = 4
ATTN_VDIM = 128
ATTN_QKDIM = 64
ROPE_HALF = ATTN_QKDIM // 2
IN_COLS = 3 * CONV_WIDTH + 3 * ATTN_WIDTH
ROPE_THETA = 10000.0
PEER_HEADS = 8
PEER_NKEYS = 128
PEER_N = PEER_NKEYS * PEER_NKEYS
PEER_HALF = 128
PEER_QCOLS = PEER_HEADS * 2 * PEER_HALF
PEER_TOPK = 16
PEER_SEL = PEER_HEADS * PEER_TOPK
LN_EPS = 1e-5

V7X_LANES = 128
V7X_SUBLANES = 8
V7X_VMEM_BYTES = 64 * 1024 * 1024
MIB = 1024 * 1024

INPROJ_ROWS = 512
ATTN_Q_ROWS = 256
OUTPROJ_ROWS = 512
ROUTE_ROWS = 256
PEER_ROWS = 256
PEER_EXPERTS = 1024
PEER_GROUPS = PEER_EXPERTS // PEER_NKEYS
GATE_PITCH = PEER_NKEYS + V7X_SUBLANES
GATE_UNROLL = 32

NT_DIMS = (((1,), (1,)), ((), ()))


def _vmem_limit(nbytes):
    return int(min(V7X_VMEM_BYTES - 6 * MIB, max(32 * MIB, nbytes)))


def _layer_norm(z, g, b):
    mu = jnp.mean(z, axis=-1, keepdims=True)
    zc = z - mu
    var = jnp.mean(zc * zc, axis=-1, keepdims=True)
    return zc * lax.rsqrt(var + LN_EPS) * g + b


def _inproj_kernel(x_ref, xp_ref, xn_ref, w_ref, cw_ref, cos_ref, sin_ref,
                   yc_ref, q_ref, k_ref, v_ref, *, blocks_per_seq):
    tm = x_ref.shape[0]
    pos = lax.rem(pl.program_id(0), blocks_per_seq)
    xe = jnp.concatenate([x_ref[...], xp_ref[...], xn_ref[...]], axis=0).astype(BF16)
    xb = xe[:tm]

    def proj(v, c0, c1):
        return jnp.dot(v, w_ref[:, c0:c1], preferred_element_type=F32)

    g_b = proj(xb, 0, CONV_WIDTH)
    ue = proj(xe, CONV_WIDTH, 2 * CONV_WIDTH) * proj(xe, 2 * CONV_WIDTH, 3 * CONV_WIDTH)
    u = ue[:tm]
    u_prev = jnp.where(pos == 0, 0.0, ue[tm + V7X_SUBLANES - 1:tm + V7X_SUBLANES])
    u_next = jnp.where(pos == blocks_per_seq - 1, 0.0, ue[tm + V7X_SUBLANES:tm + V7X_SUBLANES + 1])
    row = lax.broadcasted_iota(jnp.int32, u.shape, 0)
    u_m1 = jnp.where(row == 0, u_prev, pltpu.roll(u, 1, 0))
    u_p1 = jnp.where(row == tm - 1, u_next, pltpu.roll(u, tm - 1, 0))
    conv = u_m1 * cw_ref[0:1, :] + u * cw_ref[1:2, :] + u_p1 * cw_ref[2:3, :]
    yc_ref[...] = (g_b * conv).astype(yc_ref.dtype)

    cos = cos_ref[...]
    sin = sin_ref[...]
    lane = lax.broadcasted_iota(jnp.int32, cos.shape, 1)
    first_half = jnp.bitwise_and(lane, ATTN_QKDIM - 1) < ROPE_HALF

    def rope(t):
        n = t.shape[1]
        rot = jnp.where(first_half, pltpu.roll(t, n - ROPE_HALF, 1), pltpu.roll(t, ROPE_HALF, 1))
        return t * cos + rot * sin

    c = 3 * CONV_WIDTH
    scale = 1.0 / math.sqrt(ATTN_QKDIM)
    q_ref[...] = (rope(proj(xb, c, c + ATTN_WIDTH)) * scale).astype(q_ref.dtype)
    k_ref[...] = rope(proj(xb, c + ATTN_WIDTH, c + 2 * ATTN_WIDTH)).astype(k_ref.dtype)
    v_ref[...] = proj(xb, c + 2 * ATTN_WIDTH, c + 3 * ATTN_WIDTH).astype(v_ref.dtype)


def _inproj(x2d, w_in, conv_w, cos, sin, seq):
    n = x2d.shape[0]
    tm = min(INPROJ_ROWS, seq)
    bps = seq // tm
    nblk = n // tm
    r8 = tm // V7X_SUBLANES
    last8 = n // V7X_SUBLANES - 1
    row_spec = lambda w: pl.BlockSpec((tm, w), lambda i: (i, 0))
    tab_spec = pl.BlockSpec((tm, ATTN_WIDTH), lambda i: (lax.rem(i, bps), 0))
    out = jax.ShapeDtypeStruct((n, CONV_WIDTH), BF16)
    est = 2 * (tm * D_MODEL * 4 + D_MODEL * IN_COLS * 2 + 2 * tm * ATTN_WIDTH * 4 + 4 * tm * 512 * 2) \
        + 12 * tm * 512 * 4 + 2 * tm * D_MODEL * 4
    return pl.pallas_call(
        functools.partial(_inproj_kernel, blocks_per_seq=bps),
        grid=(nblk,),
        in_specs=[
            row_spec(D_MODEL),
            pl.BlockSpec((V7X_SUBLANES, D_MODEL), lambda i: (jnp.maximum(i * r8 - 1, 0), 0)),
            pl.BlockSpec((V7X_SUBLANES, D_MODEL), lambda i: (jnp.minimum((i + 1) * r8, last8), 0)),
            pl.BlockSpec((D_MODEL, IN_COLS), lambda i: (0, 0)),
            pl.BlockSpec((3, CONV_WIDTH), lambda i: (0, 0)),
            tab_spec, tab_spec,
        ],
        out_specs=[row_spec(CONV_WIDTH)] * 4,
        out_shape=[out] * 4,
        compiler_params=pltpu.CompilerParams(
            dimension_semantics=("parallel",), vmem_limit_bytes=_vmem_limit(est)),
        name="inproj_conv_rope",
    )(x2d, x2d, x2d, w_in, conv_w, cos, sin)


def _attn_kernel(q_ref, k_ref, v_ref, lq1_ref, lk1_ref, lq2_ref, lk2_ref, g_ref, o_ref, *, lam_init):
    q = q_ref[...]
    k = k_ref[...]
    v = v_ref[...]
    lane = lax.broadcasted_iota(jnp.int32, q.shape, 1)
    zero = jnp.zeros_like(q)

    def softmax_pv(qm):
        sc = lax.dot_general(qm, k, NT_DIMS, preferred_element_type=F32)
        m = jnp.max(sc, axis=-1, keepdims=True)
        e = jnp.exp(sc - m)
        l = jnp.sum(e, axis=-1, keepdims=True)
        return jnp.dot(e.astype(v.dtype), v, preferred_element_type=F32) / l

    lam = (jnp.exp(jnp.sum(lq1_ref[...] * lk1_ref[...], axis=-1, keepdims=True))
           - jnp.exp(jnp.sum(lq2_ref[...] * lk2_ref[...], axis=-1, keepdims=True)) + lam_init)
    o = softmax_pv(jnp.where(lane < ATTN_QKDIM, q, zero)) - lam * softmax_pv(jnp.where(lane >= ATTN_QKDIM, q, zero))
    o = o * lax.rsqrt(jnp.mean(o * o, axis=-1, keepdims=True) + LN_EPS) * g_ref[...]
    o_ref[...] = (o * (1.0 - lam_init)).astype(o_ref.dtype)


def _attention(q, k, v, lq1, lk1, lq2, lk2, subln_g, lam_init):
    b, s, _ = q.shape
    tq = min(ATTN_Q_ROWS, s)
    q_spec = pl.BlockSpec((None, tq, ATTN_VDIM), lambda bi, h, i: (bi, i, h))
    kv_spec = pl.BlockSpec((None, s, ATTN_VDIM), lambda bi, h, i: (bi, 0, h))
    vec = lambda w: pl.BlockSpec((1, w), lambda bi, h, i: (0, 0))
    est = 4 * s * ATTN_VDIM * 2 + 2 * tq * s * (4 + 4 + 2) + 8 * tq * ATTN_VDIM * 4
    return pl.pallas_call(
        functools.partial(_attn_kernel, lam_init=lam_init),
        grid=(b, N_ATTN_HEADS, s // tq),
        in_specs=[q_spec, kv_spec, kv_spec, vec(ATTN_QKDIM), vec(ATTN_QKDIM), vec(ATTN_QKDIM),
                  vec(ATTN_QKDIM), vec(ATTN_VDIM)],
        out_specs=q_spec,
        out_shape=jax.ShapeDtypeStruct((b, s, ATTN_WIDTH), BF16),
        compiler_params=pltpu.CompilerParams(
            dimension_semantics=("parallel", "parallel", "parallel"), vmem_limit_bytes=_vmem_limit(est)),
        name="diff_attention",
    )(q, k, v, lq1, lk1, lq2, lk2, subln_g)


def _outproj_kernel(yc_ref, ya_ref, x_ref, w_ref, g_ref, b_ref, o_ref, *, alpha):
    m = jnp.dot(yc_ref[...], w_ref[:CONV_WIDTH, :], preferred_element_type=F32)
    m = m + jnp.dot(ya_ref[...], w_ref[CONV_WIDTH:, :], preferred_element_type=F32)
    o_ref[...] = _layer_norm(alpha * x_ref[...] + m, g_ref[...], b_ref[...])


def _outproj(yc, ya, x2d, w_out, g, b, alpha):
    n = x2d.shape[0]
    tm = min(OUTPROJ_ROWS, n)
    row_spec = lambda w: pl.BlockSpec((tm, w), lambda i: (i, 0))
    vec = pl.BlockSpec((1, D_MODEL), lambda i: (0, 0))
    est = 2 * (2 * tm * 512 * 2 + 2 * tm * D_MODEL * 4 + D_MODEL * D_MODEL * 2) + 4 * tm * D_MODEL * 4
    return pl.pallas_call(
        functools.partial(_outproj_kernel, alpha=alpha),
        grid=(n // tm,),
        in_specs=[row_spec(CONV_WIDTH), row_spec(ATTN_WIDTH), row_spec(D_MODEL),
                  pl.BlockSpec((D_MODEL, D_MODEL), lambda i: (0, 0)), vec, vec],
        out_specs=row_spec(D_MODEL),
        out_shape=jax.ShapeDtypeStruct((n, D_MODEL), F32),
        compiler_params=pltpu.CompilerParams(
            dimension_semantics=("parallel",), vmem_limit_bytes=_vmem_limit(est)),
        name="outproj_ln",
    )(yc, ya, x2d, w_out, g, b)


def _top16(s, store_val, store_pos):
    n = s.shape[0]
    rows = lax.broadcasted_iota(jnp.int32, s.shape, 0).astype(F32)
    for r in range(PEER_TOPK):
        m = jnp.max(s, axis=0, keepdims=True)
        pos = jnp.min(jnp.where(s == m, rows, float(n)), axis=0, keepdims=True)
        s = jnp.where(rows == pos, -jnp.inf, s)
        store_val(r, m)
        store_pos(r, pos)


def _route_kernel(x_ref, wq_ref, keys_ref, i1_ref, i2_ref, g_ref,
                  sc_ref, tv_ref, tp_ref, sel1_ref, sel2_ref, gate_ref):
    tm = x_ref.shape[0]
    q = jnp.dot(x_ref[...].astype(BF16), wq_ref[...], preferred_element_type=F32)
    for hp in range(2 * PEER_HEADS):
        qhp = q[:, hp * PEER_HALF:(hp + 1) * PEER_HALF].astype(BF16)
        sc_ref[hp] = lax.dot_general(keys_ref[hp % 2], qhp, NT_DIMS, preferred_element_type=F32)

    def store(ref, slot):
        def f(r, val):
            ref[slot, r:r + 1, :] = val
        return f

    def head_body(h, carry):
        for c in range(tm // V7X_LANES):
            lanes = slice(c * V7X_LANES, (c + 1) * V7X_LANES)
            for p in range(2):
                _top16(sc_ref[2 * h + p, :, lanes], store(tv_ref, p), store(tp_ref, p))
            s1 = tv_ref[0]
            s2 = tv_ref[1]
            comb = jnp.concatenate([s1[a:a + 1, :] + s2 for a in range(PEER_TOPK)], axis=0)
            _top16(comb, store(tv_ref, 2), store(tp_ref, 2))
            top_s = tv_ref[2]
            pos = tp_ref[2].astype(jnp.int32)
            a_sel = jnp.right_shift(pos, 4)
            b_sel = jnp.bitwise_and(pos, PEER_TOPK - 1)
            i1 = tp_ref[0]
            i2 = tp_ref[1]
            e1 = jnp.zeros_like(top_s)
            e2 = jnp.zeros_like(top_s)
            for a in range(PEER_TOPK):
                e1 = jnp.where(a_sel == a, i1[a:a + 1, :], e1)
                e2 = jnp.where(b_sel == a, i2[a:a + 1, :], e2)
            ex = jnp.exp(top_s - top_s[0:1, :])
            gate = ex / jnp.sum(ex, axis=0, keepdims=True)
            base = pl.multiple_of(h * PEER_TOPK, PEER_TOPK)
            sel1_ref[pl.ds(base, PEER_TOPK), lanes] = e1
            sel2_ref[pl.ds(base, PEER_TOPK), lanes] = e2
            gate_ref[pl.ds(base, PEER_TOPK), lanes] = gate
        return carry

    lax.fori_loop(0, PEER_HEADS, head_body, 0)
    for c in range(tm // V7X_LANES):
        lanes = slice(c * V7X_LANES, (c + 1) * V7X_LANES)
        rows = slice(c * V7X_LANES, (c + 1) * V7X_LANES)
        i1_ref[rows, :] = sel1_ref[:, lanes].T
        i2_ref[rows, :] = sel2_ref[:, lanes].T
        g_ref[rows, :] = gate_ref[:, lanes].T


def _route(x2d, wq, keys):
    n = x2d.shape[0]
    tm = min(ROUTE_ROWS, n)
    out = jax.ShapeDtypeStruct((n, PEER_SEL), F32)
    out_spec = pl.BlockSpec((tm, PEER_SEL), lambda i: (i, 0))
    est = 2 * (tm * D_MODEL * 4 + D_MODEL * PEER_QCOLS * 2) + 3 * tm * PEER_QCOLS * 4 + 8 * tm * PEER_SEL * 4
    return pl.pallas_call(
        _route_kernel,
        grid=(n // tm,),
        in_specs=[pl.BlockSpec((tm, D_MODEL), lambda i: (i, 0)),
                  pl.BlockSpec((D_MODEL, PEER_QCOLS), lambda i: (0, 0)),
                  pl.BlockSpec((2, PEER_NKEYS, PEER_HALF), lambda i: (0, 0, 0))],
        out_specs=[out_spec] * 3,
        out_shape=[out] * 3,
        scratch_shapes=[pltpu.VMEM((2 * PEER_HEADS, PEER_NKEYS, tm), F32),
                        pltpu.VMEM((3, PEER_TOPK, V7X_LANES), F32),
                        pltpu.VMEM((3, PEER_TOPK, V7X_LANES), F32),
                        pltpu.VMEM((PEER_SEL, tm), F32),
                        pltpu.VMEM((PEER_SEL, tm), F32),
                        pltpu.VMEM((PEER_SEL, tm), F32)],
        compiler_params=pltpu.CompilerParams(
            dimension_semantics=("parallel",), vmem_limit_bytes=_vmem_limit(est)),
        name="peer_route",
    )(x2d, wq, keys)


def _peer_kernel(x_ref, i1_ref, i2_ref, g_ref, ut_ref, v_ref, lng_ref, lnb_ref, o_ref,
                 xb_ref, gate_ref, acc_ref, *, alpha):
    tm = x_ref.shape[0]
    j = pl.program_id(1)

    @pl.when(j == 0)
    def _():
        xb_ref[...] = x_ref[...].astype(BF16)
        acc_ref[...] = jnp.zeros_like(acc_ref)
        key_row = lax.broadcasted_iota(jnp.int32, (PEER_NKEYS, PEER_SEL), 0).astype(F32)

        def token_body(t, carry):
            r1 = i1_ref[pl.ds(t, 1), :]
            r2 = i2_ref[pl.ds(t, 1), :]
            gg = g_ref[pl.ds(t, 1), :]
            hot1 = jnp.where(key_row == r1, 1.0, 0.0).astype(BF16)
            wgt2 = jnp.where(key_row == r2, gg, 0.0).astype(BF16)
            tile = lax.dot_general(hot1, wgt2, NT_DIMS, preferred_element_type=F32)
            gate_ref[pl.ds(pl.multiple_of(t * GATE_PITCH, V7X_SUBLANES), PEER_NKEYS), :] = tile
            return carry

        lax.fori_loop(0, tm, token_body, 0, unroll=GATE_UNROLL)

    h = jnp.dot(xb_ref[...], ut_ref[...], preferred_element_type=F32)
    parts = []
    for r in range(PEER_GROUPS):
        gate = gate_ref[pl.ds(j * PEER_GROUPS + r, tm, stride=GATE_PITCH), :]
        hr = h[:, r * PEER_NKEYS:(r + 1) * PEER_NKEYS]
        gelu = 0.5 * hr * (1.0 + lax.erf(hr * math.sqrt(0.5)))
        parts.append((gelu * gate).astype(BF16))
    a = jnp.concatenate(parts, axis=1)
    acc_ref[...] += jnp.dot(a, v_ref[...], preferred_element_type=F32)

    @pl.when(j == pl.num_programs(1) - 1)
    def _():
        o_ref[...] = _layer_norm(alpha * x_ref[...] + acc_ref[...], lng_ref[...], lnb_ref[...])


def _peer_mix(x2d, i1, i2, g, u_t, v_tab, ln_g, ln_b, alpha):
    n = x2d.shape[0]
    tm = min(PEER_ROWS, n)
    row_spec = lambda w: pl.BlockSpec((tm, w), lambda i, j: (i, 0))
    vec = pl.BlockSpec((1, D_MODEL), lambda i, j: (0, 0))
    gate_bytes = tm * GATE_PITCH * PEER_NKEYS * 4
    est = gate_bytes + 2 * (2 * tm * D_MODEL * 4 + 3 * tm * PEER_SEL * 4 + 2 * PEER_EXPERTS * D_MODEL * 2) \
        + tm * D_MODEL * 6 + 4 * tm * PEER_EXPERTS * 4
    return pl.pallas_call(
        functools.partial(_peer_kernel, alpha=alpha),
        grid=(n // tm, PEER_N // PEER_EXPERTS),
        in_specs=[row_spec(D_MODEL), row_spec(PEER_SEL), row_spec(PEER_SEL), row_spec(PEER_SEL),
                  pl.BlockSpec((D_MODEL, PEER_EXPERTS), lambda i, j: (0, j)),
                  pl.BlockSpec((PEER_EXPERTS, D_MODEL), lambda i, j: (j, 0)),
                  vec, vec],
        out_specs=row_spec(D_MODEL),
        out_shape=jax.ShapeDtypeStruct((n, D_MODEL), F32),
        scratch_shapes=[pltpu.VMEM((tm, D_MODEL), BF16),
                        pltpu.VMEM((tm * GATE_PITCH, PEER_NKEYS), F32),
                        pltpu.VMEM((tm, D_MODEL), F32)],
        compiler_params=pltpu.CompilerParams(
            dimension_semantics=("parallel", "arbitrary"), vmem_limit_bytes=_vmem_limit(est)),
        name="peer_mix_ln",
    )(x2d, i1, i2, g, u_t, v_tab, ln_g, ln_b)


def _rope_tables(seq):
    inv_freq = ROPE_THETA ** (-jnp.arange(0, ATTN_QKDIM, 2, dtype=F32) / ATTN_QKDIM)
    ang = jnp.arange(seq, dtype=F32)[:, None] * inv_freq[None, :]
    ang = jnp.concatenate([ang, ang], axis=-1)
    sign = jnp.where(jnp.arange(ATTN_QKDIM) < ROPE_HALF, -1.0, 1.0).astype(F32)
    reps = ATTN_WIDTH // ATTN_QKDIM
    return jnp.tile(jnp.cos(ang), (1, reps)), jnp.tile(jnp.sin(ang) * sign, (1, reps))


def _trunk(x, layers, depth):
    b, s, _ = x.shape
    alpha = (2 * depth) ** 0.25
    cos, sin = _rope_tables(s)
    x2d = x.reshape(b * s, D_MODEL)
    for l, p in enumerate(layers):
        lam_init = 0.8 - 0.6 * math.exp(-0.3 * l)
        yc, q, k, v = _inproj(x2d, p["w_in"], p["conv_w"], cos, sin, s)
        shp = (b, s, ATTN_WIDTH)
        ya = _attention(q.reshape(shp), k.reshape(shp), v.reshape(shp),
                        p["lq1"], p["lk1"], p["lq2"], p["lk2"], p["subln_g"], lam_init)
        x2d = _outproj(yc, ya.reshape(b * s, ATTN_WIDTH), x2d, p["w_out"], p["ln1_g"], p["ln1_b"], alpha)
        i1, i2, g = _route(x2d, p["wq"], p["keys"])
        x2d = _peer_mix(x2d, i1, i2, g, p["u_t"], p["v"], p["ln2_g"], p["ln2_b"], alpha)
    return x2d.reshape(b, s, D_MODEL)


def kernel(x_prompt, x_sample, w_in, w_out, conv_w, lam_q1, lam_k1, lam_q2, lam_k2, subln_g, ln1_g, ln1_b,
           peer_wq, peer_keys, peer_u, peer_v, ln2_g, ln2_b):
    depth = w_in.shape[0]
    layers = []
    for l in range(depth):
        row = lambda a: a[l][None, :]
        layers.append(dict(
            w_in=w_in[l].astype(BF16), w_out=w_out[l].astype(BF16), conv_w=conv_w[l],
            lq1=row(lam_q1), lk1=row(lam_k1), lq2=row(lam_q2), lk2=row(lam_k2), subln_g=row(subln_g),
            ln1_g=row(ln1_g), ln1_b=row(ln1_b), ln2_g=row(ln2_g), ln2_b=row(ln2_b),
            wq=peer_wq[l].astype(BF16), keys=peer_keys[l].astype(BF16),
            u_t=peer_u[l].astype(BF16).T, v=peer_v[l].astype(BF16)))
    return (_trunk(x_prompt, layers, depth), _trunk(x_sample, layers, depth))
```

```python
import functools
import math

import jax
import jax.numpy as jnp
from jax import lax
from jax.experimental import pallas as pl
from jax.experimental.pallas import tpu as pltpu

F32 = jnp.float32
BF16 = jnp.bfloat16

D_MODEL = 1024
CONV_WIDTH = 512
ATTN_WIDTH = 512
N_ATTN_HEADS = 4
ATTN_VDIM = 128
ATTN_QKDIM = 64
ROPE_HALF = ATTN_QKDIM // 2
IN_COLS = 3 * CONV_WIDTH + 3 * ATTN_WIDTH
ROPE_THETA = 10000.0
PEER_HEADS = 8
PEER_NKEYS = 128
PEER_N = PEER_NKEYS * PEER_NKEYS
PEER_HALF = 128
PEER_QCOLS = PEER_HEADS * 2 * PEER_HALF
PEER_TOPK = 16
PEER_SEL = PEER_HEADS * PEER_TOPK
LN_EPS = 1e-5

V7X_LANES = 128
V7X_SUBLANES = 8
V7X_VMEM_BYTES = 64 * 1024 * 1024
MIB = 1024 * 1024

INPROJ_ROWS = 512
ATTN_Q_ROWS = 256
OUTPROJ_ROWS = 512
ROUTE_ROWS = 256
PEER_ROWS = 512
PEER_SLAB = 512
PEER_GROUPS = PEER_SLAB // PEER_NKEYS
PEER_KEY_HALF = PEER_NKEYS // 2
PEER_STEPS = PEER_KEY_HALF // PEER_GROUPS
GATE_PITCH = PEER_KEY_HALF + V7X_SUBLANES
GATE_UNROLL = 32

NT_DIMS = (((1,), (1,)), ((), ()))


def _vmem_limit(nbytes):
    return int(min(V7X_VMEM_BYTES - 6 * MIB, max(32 * MIB, nbytes)))


def _layer_norm(z, g, b):
    mu = jnp.mean(z, axis=-1, keepdims=True)
    zc = z - mu
    var = jnp.mean(zc * zc, axis=-1, keepdims=True)
    return zc * lax.rsqrt(var + LN_EPS) * g + b


def _inproj_kernel(x_ref, xp_ref, xn_ref, w_ref, cw_ref, cos_ref, sin_ref,
                   yc_ref, q_ref, k_ref, v_ref, *, blocks_per_seq):
    tm = x_ref.shape[0]
    pos = lax.rem(pl.program_id(0), blocks_per_seq)
    xe = jnp.concatenate([x_ref[...], xp_ref[...], xn_ref[...]], axis=0).astype(BF16)
    xb = xe[:tm]

    def proj(v, c0, c1):
        return jnp.dot(v, w_ref[:, c0:c1], preferred_element_type=F32)

    g_b = proj(xb, 0, CONV_WIDTH)
    ue = proj(xe, CONV_WIDTH, 2 * CONV_WIDTH) * proj(xe, 2 * CONV_WIDTH, 3 * CONV_WIDTH)
    u = ue[:tm]
    u_prev = jnp.where(pos == 0, 0.0, ue[tm + V7X_SUBLANES - 1:tm + V7X_SUBLANES])
    u_next = jnp.where(pos == blocks_per_seq - 1, 0.0, ue[tm + V7X_SUBLANES:tm + V7X_SUBLANES + 1])
    row = lax.broadcasted_iota(jnp.int32, u.shape, 0)
    u_m1 = jnp.where(row == 0, u_prev, pltpu.roll(u, 1, 0))
    u_p1 = jnp.where(row == tm - 1, u_next, pltpu.roll(u, tm - 1, 0))
    conv = u_m1 * cw_ref[0:1, :] + u * cw_ref[1:2, :] + u_p1 * cw_ref[2:3, :]
    yc_ref[...] = (g_b * conv).astype(yc_ref.dtype)

    cos = cos_ref[...]
    sin = sin_ref[...]
    lane = lax.broadcasted_iota(jnp.int32, cos.shape, 1)
    first_half = jnp.bitwise_and(lane, ATTN_QKDIM - 1) < ROPE_HALF

    def rope(t):
        n = t.shape[1]
        rot = jnp.where(first_half, pltpu.roll(t, n - ROPE_HALF, 1), pltpu.roll(t, ROPE_HALF, 1))
        return t * cos + rot * sin

    c = 3 * CONV_WIDTH
    scale = 1.0 / math.sqrt(ATTN_QKDIM)
    q_ref[...] = (rope(proj(xb, c, c + ATTN_WIDTH)) * scale).astype(q_ref.dtype)
    k_ref[...] = rope(proj(xb, c + ATTN_WIDTH, c + 2 * ATTN_WIDTH)).astype(k_ref.dtype)
    v_ref[...] = proj(xb, c + 2 * ATTN_WIDTH, c + 3 * ATTN_WIDTH).astype(v_ref.dtype)


def _inproj(x2d, w_in, conv_w, cos, sin, seq):
    n = x2d.shape[0]
    tm = min(INPROJ_ROWS, seq)
    bps = seq // tm
    nblk = n // tm
    r8 = tm // V7X_SUBLANES
    last8 = n // V7X_SUBLANES - 1
    row_spec = lambda w: pl.BlockSpec((tm, w), lambda i: (i, 0))
    tab_spec = pl.BlockSpec((tm, ATTN_WIDTH), lambda i: (lax.rem(i, bps), 0))
    out = jax.ShapeDtypeStruct((n, CONV_WIDTH), BF16)
    est = 2 * (tm * D_MODEL * 4 + D_MODEL * IN_COLS * 2 + 2 * tm * ATTN_WIDTH * 4 + 4 * tm * 512 * 2) \
        + 12 * tm * 512 * 4 + 2 * tm * D_MODEL * 4
    return pl.pallas_call(
        functools.partial(_inproj_kernel, blocks_per_seq=bps),
        grid=(nblk,),
        in_specs=[
            row_spec(D_MODEL),
            pl.BlockSpec((V7X_SUBLANES, D_MODEL), lambda i: (jnp.maximum(i * r8 - 1, 0), 0)),
            pl.BlockSpec((V7X_SUBLANES, D_MODEL), lambda i: (jnp.minimum((i + 1) * r8, last8), 0)),
            pl.BlockSpec((D_MODEL, IN_COLS), lambda i: (0, 0)),
            pl.BlockSpec((3, CONV_WIDTH), lambda i: (0, 0)),
            tab_spec, tab_spec,
        ],
        out_specs=[row_spec(CONV_WIDTH)] * 4,
        out_shape=[out] * 4,
        compiler_params=pltpu.CompilerParams(
            dimension_semantics=("parallel",), vmem_limit_bytes=_vmem_limit(est)),
        name="inproj_conv_rope",
    )(x2d, x2d, x2d, w_in, conv_w, cos, sin)


def _attn_kernel(q_ref, k_ref, v_ref, lq1_ref, lk1_ref, lq2_ref, lk2_ref, g_ref, o_ref, *, lam_init):
    q = q_ref[...]
    k = k_ref[...]
    v = v_ref[...]
    lane = lax.broadcasted_iota(jnp.int32, q.shape, 1)
    zero = jnp.zeros_like(q)

    def softmax_pv(qm):
        sc = lax.dot_general(qm, k, NT_DIMS, preferred_element_type=F32)
        m = jnp.max(sc, axis=-1, keepdims=True)
        e = jnp.exp(sc - m)
        l = jnp.sum(e, axis=-1, keepdims=True)
        return jnp.dot(e.astype(v.dtype), v, preferred_element_type=F32) / l

    lam = (jnp.exp(jnp.sum(lq1_ref[...] * lk1_ref[...], axis=-1, keepdims=True))
           - jnp.exp(jnp.sum(lq2_ref[...] * lk2_ref[...], axis=-1, keepdims=True)) + lam_init)
    o = softmax_pv(jnp.where(lane < ATTN_QKDIM, q, zero)) - lam * softmax_pv(jnp.where(lane >= ATTN_QKDIM, q, zero))
    o = o * lax.rsqrt(jnp.mean(o * o, axis=-1, keepdims=True) + LN_EPS) * g_ref[...]
    o_ref[...] = (o * (1.0 - lam_init)).astype(o_ref.dtype)


def _attention(q, k, v, lq1, lk1, lq2, lk2, subln_g, lam_init):
    b, s, _ = q.shape
    tq = min(ATTN_Q_ROWS, s)
    q_spec = pl.BlockSpec((None, tq, ATTN_VDIM), lambda bi, h, i: (bi, i, h))
    kv_spec = pl.BlockSpec((None, s, ATTN_VDIM), lambda bi, h, i: (bi, 0, h))
    vec = lambda w: pl.BlockSpec((1, w), lambda bi, h, i: (0, 0))
    est = 4 * s * ATTN_VDIM * 2 + 2 * tq * s * (4 + 4 + 2) + 8 * tq * ATTN_VDIM * 4
    return pl.pallas_call(
        functools.partial(_attn_kernel, lam_init=lam_init),
        grid=(b, N_ATTN_HEADS, s // tq),
        in_specs=[q_spec, kv_spec, kv_spec, vec(ATTN_QKDIM), vec(ATTN_QKDIM), vec(ATTN_QKDIM),
                  vec(ATTN_QKDIM), vec(ATTN_VDIM)],
        out_specs=q_spec,
        out_shape=jax.ShapeDtypeStruct((b, s, ATTN_WIDTH), BF16),
        compiler_params=pltpu.CompilerParams(
            dimension_semantics=("parallel", "parallel", "parallel"), vmem_limit_bytes=_vmem_limit(est)),
        name="diff_attention",
    )(q, k, v, lq1, lk1, lq2, lk2, subln_g)


def _outproj_kernel(yc_ref, ya_ref, x_ref, w_ref, g_ref, b_ref, o_ref, *, alpha):
    m = jnp.dot(yc_ref[...], w_ref[:CONV_WIDTH, :], preferred_element_type=F32)
    m = m + jnp.dot(ya_ref[...], w_ref[CONV_WIDTH:, :], preferred_element_type=F32)
    o_ref[...] = _layer_norm(alpha * x_ref[...] + m, g_ref[...], b_ref[...])


def _outproj(yc, ya, x2d, w_out, g, b, alpha):
    n = x2d.shape[0]
    tm = min(OUTPROJ_ROWS, n)
    row_spec = lambda w: pl.BlockSpec((tm, w), lambda i: (i, 0))
    vec = pl.BlockSpec((1, D_MODEL), lambda i: (0, 0))
    est = 2 * (2 * tm * 512 * 2 + 2 * tm * D_MODEL * 4 + D_MODEL * D_MODEL * 2) + 4 * tm * D_MODEL * 4
    return pl.pallas_call(
        functools.partial(_outproj_kernel, alpha=alpha),
        grid=(n // tm,),
        in_specs=[row_spec(CONV_WIDTH), row_spec(ATTN_WIDTH), row_spec(D_MODEL),
                  pl.BlockSpec((D_MODEL, D_MODEL), lambda i: (0, 0)), vec, vec],
        out_specs=row_spec(D_MODEL),
        out_shape=jax.ShapeDtypeStruct((n, D_MODEL), F32),
        compiler_params=pltpu.CompilerParams(
            dimension_semantics=("parallel",), vmem_limit_bytes=_vmem_limit(est)),
        name="outproj_ln",
    )(yc, ya, x2d, w_out, g, b)


def _row_index(n, lanes):
    return lax.broadcasted_iota(jnp.int32, (n, lanes), 0).astype(F32)


def _top16(s, keys, store_val, store_key):
    for r in range(PEER_TOPK):
        m = jnp.max(s, axis=0, keepdims=True)
        key = jnp.min(jnp.where(s == m, keys, jnp.inf), axis=0, keepdims=True)
        s = jnp.where(keys == key, -jnp.inf, s)
        store_val(r, m)
        store_key(r, key)


def _pair_candidates(s1, s2):
    lanes = s1.shape[1]
    row16 = _row_index(PEER_TOPK, lanes)
    row8 = row16[:V7X_SUBLANES]
    sums, keys = [], []
    for a in range(2):
        sums.append(s1[a:a + 1, :] + s2)
        keys.append(row16 + float(a * PEER_TOPK))
    s1_tail = jnp.where(row16 >= 2.0, s1, -jnp.inf)
    for b in range(2):
        sums.append(s1_tail + s2[b:b + 1, :])
        keys.append(row16 * float(PEER_TOPK) + float(b))
    for b in range(2, 5):
        sums.append(s1_tail[:V7X_SUBLANES] + s2[b:b + 1, :])
        keys.append(row8 * float(PEER_TOPK) + float(b))
    return jnp.concatenate(sums, axis=0), jnp.concatenate(keys, axis=0)


def _route_kernel(x_ref, wq_ref, keys_ref, i1_ref, i2_ref, g_ref,
                  sc_ref, tv_ref, tp_ref, sel1_ref, sel2_ref, gate_ref):
    tm = x_ref.shape[0]
    q = jnp.dot(x_ref[...].astype(BF16), wq_ref[...], preferred_element_type=F32)
    for hp in range(2 * PEER_HEADS):
        qhp = q[:, hp * PEER_HALF:(hp + 1) * PEER_HALF].astype(BF16)
        sc_ref[hp] = lax.dot_general(keys_ref[hp % 2], qhp, NT_DIMS, preferred_element_type=F32)

    def store(ref, slot):
        def f(r, val):
            ref[slot, r:r + 1, :] = val
        return f

    def head_body(h, carry):
        for c in range(tm // V7X_LANES):
            lanes = slice(c * V7X_LANES, (c + 1) * V7X_LANES)
            for p in range(2):
                _top16(sc_ref[2 * h + p, :, lanes], _row_index(PEER_NKEYS, V7X_LANES),
                       store(tv_ref, p), store(tp_ref, p))
            comb, comb_keys = _pair_candidates(tv_ref[0], tv_ref[1])
            _top16(comb, comb_keys, store(tv_ref, 2), store(tp_ref, 2))
            top_s = tv_ref[2]
            pos = tp_ref[2].astype(jnp.int32)
            a_sel = jnp.right_shift(pos, 4)
            b_sel = jnp.bitwise_and(pos, PEER_TOPK - 1)
            i1 = tp_ref[0]
            i2 = tp_ref[1]
            e1 = jnp.zeros_like(top_s)
            e2 = jnp.zeros_like(top_s)
            for a in range(PEER_TOPK):
                e1 = jnp.where(a_sel == a, i1[a:a + 1, :], e1)
                e2 = jnp.where(b_sel == a, i2[a:a + 1, :], e2)
            ex = jnp.exp(top_s - top_s[0:1, :])
            gate = ex / jnp.sum(ex, axis=0, keepdims=True)
            base = pl.multiple_of(h * PEER_TOPK, PEER_TOPK)
            sel1_ref[pl.ds(base, PEER_TOPK), lanes] = e1
            sel2_ref[pl.ds(base, PEER_TOPK), lanes] = e2
            gate_ref[pl.ds(base, PEER_TOPK), lanes] = gate
        return carry

    lax.fori_loop(0, PEER_HEADS, head_body, 0)
    for c in range(tm // V7X_LANES):
        lanes = slice(c * V7X_LANES, (c + 1) * V7X_LANES)
        rows = slice(c * V7X_LANES, (c + 1) * V7X_LANES)
        i1_ref[rows, :] = sel1_ref[:, lanes].T
        i2_ref[rows, :] = sel2_ref[:, lanes].T
        g_ref[rows, :] = gate_ref[:, lanes].T


def _route(x2d, wq, keys):
    n = x2d.shape[0]
    tm = min(ROUTE_ROWS, n)
    out = jax.ShapeDtypeStruct((n, PEER_SEL), F32)
    out_spec = pl.BlockSpec((tm, PEER_SEL), lambda i: (i, 0))
    est = 2 * (tm * D_MODEL * 4 + D_MODEL * PEER_QCOLS * 2) + 3 * tm * PEER_QCOLS * 4 + 8 * tm * PEER_SEL * 4
    return pl.pallas_call(
        _route_kernel,
        grid=(n // tm,),
        in_specs=[pl.BlockSpec((tm, D_MODEL), lambda i: (i, 0)),
                  pl.BlockSpec((D_MODEL, PEER_QCOLS), lambda i: (0, 0)),
                  pl.BlockSpec((2, PEER_NKEYS, PEER_HALF), lambda i: (0, 0, 0))],
        out_specs=[out_spec] * 3,
        out_shape=[out] * 3,
        scratch_shapes=[pltpu.VMEM((2 * PEER_HEADS, PEER_NKEYS, tm), F32),
                        pltpu.VMEM((3, PEER_TOPK, V7X_LANES), F32),
                        pltpu.VMEM((3, PEER_TOPK, V7X_LANES), F32),
                        pltpu.VMEM((PEER_SEL, tm), F32),
                        pltpu.VMEM((PEER_SEL, tm), F32),
                        pltpu.VMEM((PEER_SEL, tm), F32)],
        compiler_params=pltpu.CompilerParams(
            dimension_semantics=("parallel",), vmem_limit_bytes=_vmem_limit(est)),
        name="peer_route",
    )(x2d, wq, keys)


def _peer_kernel(x_ref, i1_ref, i2_ref, g_ref, ut_lo_ref, ut_hi_ref, v_lo_ref, v_hi_ref, lng_ref, lnb_ref, o_ref,
                 xb_ref, gate_ref, acc_ref, *, alpha):
    tm = x_ref.shape[0]
    j = pl.program_id(1)

    @pl.when(j == 0)
    def _():
        xb_ref[...] = x_ref[...].astype(BF16)
        acc_ref[...] = jnp.zeros_like(acc_ref)
        key_row = lax.broadcasted_iota(jnp.int32, (PEER_NKEYS, PEER_SEL), 0).astype(F32)

        def token_body(t, carry):
            r1 = i1_ref[pl.ds(t, 1), :]
            r2 = i2_ref[pl.ds(t, 1), :]
            gg = g_ref[pl.ds(t, 1), :]
            hot1 = jnp.where(key_row == r1, 1.0, 0.0).astype(BF16)
            wgt2 = jnp.where(key_row == r2, gg, 0.0).astype(BF16)
            tile = lax.dot_general(hot1, wgt2, NT_DIMS, preferred_element_type=F32)
            bits = lax.bitcast_convert_type(tile.astype(BF16).astype(F32), jnp.uint32)
            packed = jnp.bitwise_or(jnp.right_shift(bits[:PEER_KEY_HALF], jnp.uint32(16)), bits[PEER_KEY_HALF:])
            gate_ref[pl.ds(pl.multiple_of(t * GATE_PITCH, V7X_SUBLANES), PEER_KEY_HALF), :] = packed
            return carry

        lax.fori_loop(0, tm, token_body, 0, unroll=GATE_UNROLL)

    xb = xb_ref[...]
    h_lo = jnp.dot(xb, ut_lo_ref[...], preferred_element_type=F32)
    h_hi = jnp.dot(xb, ut_hi_ref[...], preferred_element_type=F32)
    parts_lo, parts_hi = [], []
    for r in range(PEER_GROUPS):
        word = gate_ref[pl.ds(j * PEER_GROUPS + r, tm, stride=GATE_PITCH), :]
        cols = slice(r * PEER_NKEYS, (r + 1) * PEER_NKEYS)
        gate_lo = lax.bitcast_convert_type(jnp.left_shift(word, jnp.uint32(16)), F32)
        gate_hi = lax.bitcast_convert_type(jnp.bitwise_and(word, jnp.uint32(0xFFFF0000)), F32)
        for h, gate, parts in ((h_lo, gate_lo, parts_lo), (h_hi, gate_hi, parts_hi)):
            hr = h[:, cols]
            gelu = 0.5 * hr * (1.0 + lax.erf(hr * math.sqrt(0.5)))
            parts.append((gelu * gate).astype(BF16))
    acc = jnp.dot(jnp.concatenate(parts_lo, axis=1), v_lo_ref[...], preferred_element_type=F32)
    acc = acc + jnp.dot(jnp.concatenate(parts_hi, axis=1), v_hi_ref[...], preferred_element_type=F32)
    acc_ref[...] += acc

    @pl.when(j == pl.num_programs(1) - 1)
    def _():
        o_ref[...] = _layer_norm(alpha * x_ref[...] + acc_ref[...], lng_ref[...], lnb_ref[...])


def _peer_mix(x2d, i1, i2, g, u_t, v_tab, ln_g, ln_b, alpha):
    n = x2d.shape[0]
    tm = min(PEER_ROWS, n)
    once = pl.Buffered(1)
    row_spec = lambda w: pl.BlockSpec((tm, w), lambda i, j: (i, 0), pipeline_mode=once)
    vec = pl.BlockSpec((1, D_MODEL), lambda i, j: (0, 0))
    gate_bytes = tm * GATE_PITCH * PEER_NKEYS * 4
    est = gate_bytes + 2 * tm * D_MODEL * 4 + 3 * tm * PEER_SEL * 4 + 8 * PEER_SLAB * D_MODEL * 2 \
        + tm * D_MODEL * 6 + 8 * tm * PEER_SLAB * 4
    return pl.pallas_call(
        functools.partial(_peer_kernel, alpha=alpha),
        grid=(n // tm, PEER_STEPS),
        in_specs=[row_spec(D_MODEL), row_spec(PEER_SEL), row_spec(PEER_SEL), row_spec(PEER_SEL),
                  pl.BlockSpec((D_MODEL, PEER_SLAB), lambda i, j: (0, j)),
                  pl.BlockSpec((D_MODEL, PEER_SLAB), lambda i, j: (0, j + PEER_STEPS)),
                  pl.BlockSpec((PEER_SLAB, D_MODEL), lambda i, j: (j, 0)),
                  pl.BlockSpec((PEER_SLAB, D_MODEL), lambda i, j: (j + PEER_STEPS, 0)),
                  vec, vec],
        out_specs=pl.BlockSpec((tm, D_MODEL), lambda i, j: (i, 0)),
        out_shape=jax.ShapeDtypeStruct((n, D_MODEL), F32),
        scratch_shapes=[pltpu.VMEM((tm, D_MODEL), BF16),
                        pltpu.VMEM((tm * GATE_PITCH, PEER_NKEYS), jnp.uint32),
                        pltpu.VMEM((tm, D_MODEL), F32)],
        compiler_params=pltpu.CompilerParams(
            dimension_semantics=("parallel", "arbitrary"), vmem_limit_bytes=_vmem_limit(est)),
        name="peer_mix_ln",
    )(x2d, i1, i2, g, u_t, u_t, v_tab, v_tab, ln_g, ln_b)


def _rope_tables(seq):
    inv_freq = ROPE_THETA ** (-jnp.arange(0, ATTN_QKDIM, 2, dtype=F32) / ATTN_QKDIM)
    ang = jnp.arange(seq, dtype=F32)[:, None] * inv_freq[None, :]
    ang = jnp.concatenate([ang, ang], axis=-1)
    sign = jnp.where(jnp.arange(ATTN_QKDIM) < ROPE_HALF, -1.0, 1.0).astype(F32)
    reps = ATTN_WIDTH // ATTN_QKDIM
    return jnp.tile(jnp.cos(ang), (1, reps)), jnp.tile(jnp.sin(ang) * sign, (1, reps))


def _trunk(x, layers, depth):
    b, s, _ = x.shape
    alpha = (2 * depth) ** 0.25
    cos, sin = _rope_tables(s)
    x2d = x.reshape(b * s, D_MODEL)
    for l, p in enumerate(layers):
        lam_init = 0.8 - 0.6 * math.exp(-0.3 * l)
        yc, q, k, v = _inproj(x2d, p["w_in"], p["conv_w"], cos, sin, s)
        shp = (b, s, ATTN_WIDTH)
        ya = _attention(q.reshape(shp), k.reshape(shp), v.reshape(shp),
                        p["lq1"], p["lk1"], p["lq2"], p["lk2"], p["subln_g"], lam_init)
        x2d = _outproj(yc, ya.reshape(b * s, ATTN_WIDTH), x2d, p["w_out"], p["ln1_g"], p["ln1_b"], alpha)
        i1, i2, g = _route(x2d, p["wq"], p["keys"])
        x2d = _peer_mix(x2d, i1, i2, g, p["u_t"], p["v"], p["ln2_g"], p["ln2_b"], alpha)
    return x2d.reshape(b, s, D_MODEL)


def kernel(x_prompt, x_sample, w_in, w_out, conv_w, lam_q1, lam_k1, lam_q2, lam_k2, subln_g, ln1_g, ln1_b,
           peer_wq, peer_keys, peer_u, peer_v, ln2_g, ln2_b):
    depth = w_in.shape[0]
    layers = []
    for l in range(depth):
        row = lambda a: a[l][None, :]
        layers.append(dict(
            w_in=w_in[l].astype(BF16), w_out=w_out[l].astype(BF16), conv_w=conv_w[l],
            lq1=row(lam_q1), lk1=row(lam_k1), lq2=row(lam_q2), lk2=row(lam_k2), subln_g=row(subln_g),
            ln1_g=row(ln1_g), ln1_b=row(ln1_b), ln2_g=row(ln2_g), ln2_b=row(ln2_b),
            wq=peer_wq[l].astype(BF16), keys=peer_keys[l].astype(BF16),
            u_t=peer_u[l].astype(BF16).T, v=peer_v[l].astype(BF16)))
    return (_trunk(x_prompt, layers, depth), _trunk(x_sample, layers, depth))
```

```python
import functools
import math

import jax
import jax.numpy as jnp
from jax import lax
from jax.experimental import pallas as pl
from jax.experimental.pallas import tpu as pltpu

F32 = jnp.float32
BF16 = jnp.bfloat16

D_MODEL = 1024
CONV_WIDTH = 512
ATTN_WIDTH = 512
N_ATTN_HEADS = 4
ATTN_VDIM = 128
ATTN_QKDIM = 64
ROPE_HALF = ATTN_QKDIM // 2
IN_COLS = 3 * CONV_WIDTH + 3 * ATTN_WIDTH
ROPE_THETA = 10000.0
PEER_HEADS = 8
PEER_NKEYS = 128
PEER_N = PEER_NKEYS * PEER_NKEYS
PEER_HALF = 128
PEER_QCOLS = PEER_HEADS * 2 * PEER_HALF
PEER_TOPK = 16
PEER_SEL = PEER_HEADS * PEER_TOPK
LN_EPS = 1e-5

V7X_LANES = 128
V7X_SUBLANES = 8
V7X_VMEM_BYTES = 64 * 1024 * 1024
MIB = 1024 * 1024

INPROJ_ROWS = 512
ATTN_Q_ROWS = 256
ATTN_KEY_CHUNK = 512
ATTN_LOOKAHEAD = 6
OUTPROJ_ROWS = 512
ROUTE_ROWS = 256
PEER_ROWS = 512
PEER_SLAB = 512
PEER_GROUPS = PEER_SLAB // PEER_NKEYS
PEER_KEY_HALF = PEER_NKEYS // 2
PEER_STEPS = PEER_KEY_HALF // PEER_GROUPS
GATE_PITCH = PEER_KEY_HALF + V7X_SUBLANES
GATE_UNROLL = 32

NT_DIMS = (((1,), (1,)), ((), ()))


def _vmem_limit(nbytes):
    return int(min(V7X_VMEM_BYTES - 6 * MIB, max(32 * MIB, nbytes)))


def _layer_norm(z, g, b):
    mu = jnp.mean(z, axis=-1, keepdims=True)
    zc = z - mu
    var = jnp.mean(zc * zc, axis=-1, keepdims=True)
    return zc * lax.rsqrt(var + LN_EPS) * g + b


def _inproj_kernel(x_ref, xp_ref, xn_ref, w_ref, wvt_ref, cw_ref, cos_ref, sin_ref,
                   yc_ref, q_ref, k_ref, vt_ref, *, blocks_per_seq):
    tm = x_ref.shape[0]
    pos = lax.rem(pl.program_id(0), blocks_per_seq)
    xe = jnp.concatenate([x_ref[...], xp_ref[...], xn_ref[...]], axis=0).astype(BF16)
    xb = xe[:tm]

    def proj(v, c0, c1):
        return jnp.dot(v, w_ref[:, c0:c1], preferred_element_type=F32)

    g_b = proj(xb, 0, CONV_WIDTH)
    ue = proj(xe, CONV_WIDTH, 2 * CONV_WIDTH) * proj(xe, 2 * CONV_WIDTH, 3 * CONV_WIDTH)
    u = ue[:tm]
    u_prev = jnp.where(pos == 0, 0.0, ue[tm + V7X_SUBLANES - 1:tm + V7X_SUBLANES])
    u_next = jnp.where(pos == blocks_per_seq - 1, 0.0, ue[tm + V7X_SUBLANES:tm + V7X_SUBLANES + 1])
    row = lax.broadcasted_iota(jnp.int32, u.shape, 0)
    u_m1 = jnp.where(row == 0, u_prev, pltpu.roll(u, 1, 0))
    u_p1 = jnp.where(row == tm - 1, u_next, pltpu.roll(u, tm - 1, 0))
    conv = u_m1 * cw_ref[0:1, :] + u * cw_ref[1:2, :] + u_p1 * cw_ref[2:3, :]
    yc_ref[...] = (g_b * conv).astype(yc_ref.dtype)

    cos = cos_ref[...]
    sin = sin_ref[...]
    lane = lax.broadcasted_iota(jnp.int32, cos.shape, 1)
    first_half = jnp.bitwise_and(lane, ATTN_QKDIM - 1) < ROPE_HALF

    def rope(t):
        n = t.shape[1]
        rot = jnp.where(first_half, pltpu.roll(t, n - ROPE_HALF, 1), pltpu.roll(t, ROPE_HALF, 1))
        return t * cos + rot * sin

    c = 3 * CONV_WIDTH
    scale = math.log2(math.e) / math.sqrt(ATTN_QKDIM)
    q_ref[...] = (rope(proj(xb, c, c + ATTN_WIDTH)) * scale).astype(q_ref.dtype)
    k_ref[...] = rope(proj(xb, c + ATTN_WIDTH, c + 2 * ATTN_WIDTH)).astype(k_ref.dtype)
    vt_ref[...] = lax.dot_general(wvt_ref[...], xb, NT_DIMS, preferred_element_type=F32).astype(vt_ref.dtype)


def _inproj(x2d, w_in, wv_t, conv_w, cos, sin, seq):
    n = x2d.shape[0]
    tm = min(INPROJ_ROWS, seq)
    bps = seq // tm
    nblk = n // tm
    r8 = tm // V7X_SUBLANES
    last8 = n // V7X_SUBLANES - 1
    row_spec = lambda w: pl.BlockSpec((tm, w), lambda i: (i, 0))
    tab_spec = pl.BlockSpec((tm, ATTN_WIDTH), lambda i: (lax.rem(i, bps), 0))
    out = jax.ShapeDtypeStruct((n, CONV_WIDTH), BF16)
    est = 2 * (tm * D_MODEL * 4 + D_MODEL * IN_COLS * 2 + 2 * tm * ATTN_WIDTH * 4 + 4 * tm * 512 * 2) \
        + 12 * tm * 512 * 4 + 2 * tm * D_MODEL * 4
    return pl.pallas_call(
        functools.partial(_inproj_kernel, blocks_per_seq=bps),
        grid=(nblk,),
        in_specs=[
            row_spec(D_MODEL),
            pl.BlockSpec((V7X_SUBLANES, D_MODEL), lambda i: (jnp.maximum(i * r8 - 1, 0), 0)),
            pl.BlockSpec((V7X_SUBLANES, D_MODEL), lambda i: (jnp.minimum((i + 1) * r8, last8), 0)),
            pl.BlockSpec((D_MODEL, IN_COLS), lambda i: (0, 0)),
            pl.BlockSpec((ATTN_WIDTH, D_MODEL), lambda i: (0, 0)),
            pl.BlockSpec((3, CONV_WIDTH), lambda i: (0, 0)),
            tab_spec, tab_spec,
        ],
        out_specs=[row_spec(CONV_WIDTH)] * 3 + [pl.BlockSpec((ATTN_WIDTH, tm), lambda i: (0, i))],
        out_shape=[out] * 3 + [jax.ShapeDtypeStruct((ATTN_WIDTH, n), BF16)],
        compiler_params=pltpu.CompilerParams(
            dimension_semantics=("parallel",), vmem_limit_bytes=_vmem_limit(est)),
        name="inproj_conv_rope",
    )(x2d, x2d, x2d, w_in, wv_t, conv_w, cos, sin)


def _attn_kernel(q_ref, k_ref, vt_ref, lq1_ref, lk1_ref, lq2_ref, lk2_ref, g_ref, o_ref, *, lam_init):
    q = q_ref[...]
    s = k_ref.shape[0]
    ck = min(ATTN_KEY_CHUNK, s)
    lane = lax.broadcasted_iota(jnp.int32, q.shape, 1)
    zero = jnp.zeros_like(q)
    q_maps = (jnp.where(lane < ATTN_QKDIM, q, zero), jnp.where(lane >= ATTN_QKDIM, q, zero))

    m_run, l_run, o_run = [None, None], [None, None], [None, None]
    items = [(c, mp) for c in range(s // ck) for mp in range(2)]

    def scores(item):
        c, mp = item
        return lax.dot_general(k_ref[c * ck:(c + 1) * ck, :], q_maps[mp], NT_DIMS,
                               preferred_element_type=F32)

    pending = [scores(it) for it in items[:ATTN_LOOKAHEAD]]
    for n, (c, mp) in enumerate(items):
        keys = slice(c * ck, (c + 1) * ck)
        if n + ATTN_LOOKAHEAD < len(items):
            pending.append(scores(items[n + ATTN_LOOKAHEAD]))
        st = pending.pop(0)
        m_c = jnp.max(st, axis=0, keepdims=True)
        m_new = m_c if c == 0 else jnp.maximum(m_run[mp], m_c)
        e = jnp.exp2(st - m_new)
        l_c = jnp.sum(e, axis=0, keepdims=True)
        pv = jnp.dot(vt_ref[:, keys], e.astype(vt_ref.dtype), preferred_element_type=F32)
        if c == 0:
            l_run[mp], o_run[mp] = l_c, pv
        else:
            rescale = jnp.exp2(m_run[mp] - m_new)
            l_run[mp] = l_run[mp] * rescale + l_c
            o_run[mp] = o_run[mp] * rescale + pv
        m_run[mp] = m_new

    lam = (jnp.exp(jnp.sum(lq1_ref[...] * lk1_ref[...], axis=-1, keepdims=True))
           - jnp.exp(jnp.sum(lq2_ref[...] * lk2_ref[...], axis=-1, keepdims=True)) + lam_init)
    ot = o_run[0] / l_run[0] - lam * (o_run[1] / l_run[1])
    ot = ot * lax.rsqrt(jnp.mean(ot * ot, axis=0, keepdims=True) + LN_EPS)
    o_ref[...] = (ot.T * g_ref[...] * (1.0 - lam_init)).astype(o_ref.dtype)


def _attention(q, k, vt, lq1, lk1, lq2, lk2, subln_g, lam_init):
    b, s, _ = q.shape
    tq = min(ATTN_Q_ROWS, s)
    q_spec = pl.BlockSpec((None, tq, ATTN_VDIM), lambda bi, h, i: (bi, i, h))
    k_spec = pl.BlockSpec((None, s, ATTN_VDIM), lambda bi, h, i: (bi, 0, h))
    vt_spec = pl.BlockSpec((ATTN_VDIM, s), lambda bi, h, i: (h, bi))
    vec = lambda w: pl.BlockSpec((1, w), lambda bi, h, i: (0, 0))
    est = 4 * s * ATTN_VDIM * 2 + 2 * tq * s * (4 + 4 + 2) + 8 * tq * ATTN_VDIM * 4
    return pl.pallas_call(
        functools.partial(_attn_kernel, lam_init=lam_init),
        grid=(b, N_ATTN_HEADS, s // tq),
        in_specs=[q_spec, k_spec, vt_spec, vec(ATTN_QKDIM), vec(ATTN_QKDIM), vec(ATTN_QKDIM),
                  vec(ATTN_QKDIM), vec(ATTN_VDIM)],
        out_specs=q_spec,
        out_shape=jax.ShapeDtypeStruct((b, s, ATTN_WIDTH), BF16),
        compiler_params=pltpu.CompilerParams(
            dimension_semantics=("parallel", "parallel", "parallel"), vmem_limit_bytes=_vmem_limit(est)),
        name="diff_attention",
    )(q, k, vt, lq1, lk1, lq2, lk2, subln_g)


def _outproj_kernel(yc_ref, ya_ref, x_ref, w_ref, g_ref, b_ref, o_ref, *, alpha):
    m = jnp.dot(yc_ref[...], w_ref[:CONV_WIDTH, :], preferred_element_type=F32)
    m = m + jnp.dot(ya_ref[...], w_ref[CONV_WIDTH:, :], preferred_element_type=F32)
    o_ref[...] = _layer_norm(alpha * x_ref[...] + m, g_ref[...], b_ref[...])


def _outproj(yc, ya, x2d, w_out, g, b, alpha):
    n = x2d.shape[0]
    tm = min(OUTPROJ_ROWS, n)
    row_spec = lambda w: pl.BlockSpec((tm, w), lambda i: (i, 0))
    vec = pl.BlockSpec((1, D_MODEL), lambda i: (0, 0))
    est = 2 * (2 * tm * 512 * 2 + 2 * tm * D_MODEL * 4 + D_MODEL * D_MODEL * 2) + 4 * tm * D_MODEL * 4
    return pl.pallas_call(
        functools.partial(_outproj_kernel, alpha=alpha),
        grid=(n // tm,),
        in_specs=[row_spec(CONV_WIDTH), row_spec(ATTN_WIDTH), row_spec(D_MODEL),
                  pl.BlockSpec((D_MODEL, D_MODEL), lambda i: (0, 0)), vec, vec],
        out_specs=row_spec(D_MODEL),
        out_shape=jax.ShapeDtypeStruct((n, D_MODEL), F32),
        compiler_params=pltpu.CompilerParams(
            dimension_semantics=("parallel",), vmem_limit_bytes=_vmem_limit(est)),
        name="outproj_ln",
    )(yc, ya, x2d, w_out, g, b)


def _row_index(n, lanes):
    return lax.broadcasted_iota(jnp.int32, (n, lanes), 0).astype(F32)


def _top16(s, keys, store_val, store_key):
    for r in range(PEER_TOPK):
        m = jnp.max(s, axis=0, keepdims=True)
        key = jnp.min(jnp.where(s == m, keys, jnp.inf), axis=0, keepdims=True)
        s = jnp.where(keys == key, -jnp.inf, s)
        store_val(r, m)
        store_key(r, key)


def _top16_by_row(s, store_val, store_key):
    n, lanes = s.shape
    nslab = n // V7X_SUBLANES
    row8 = _row_index(V7X_SUBLANES, lanes)
    slab_rows = [row8 + float(V7X_SUBLANES * i) for i in range(nslab)]
    rows = _row_index(n, lanes)
    for r in range(PEER_TOPK):
        vals = [s[i * V7X_SUBLANES:(i + 1) * V7X_SUBLANES, :] for i in range(nslab)]
        idxs = slab_rows
        while len(vals) > 1:
            nxt_v, nxt_i = [], []
            for a in range(0, len(vals), 2):
                left = vals[a] >= vals[a + 1]
                nxt_v.append(jnp.maximum(vals[a], vals[a + 1]))
                nxt_i.append(jnp.where(left, idxs[a], idxs[a + 1]))
            vals, idxs = nxt_v, nxt_i
        m = jnp.max(vals[0], axis=0, keepdims=True)
        key = jnp.min(jnp.where(vals[0] == m, idxs[0], jnp.inf), axis=0, keepdims=True)
        s = jnp.where(rows == key, -jnp.inf, s)
        store_val(r, m)
        store_key(r, key)


def _pair_candidates(s1, s2):
    lanes = s1.shape[1]
    row16 = _row_index(PEER_TOPK, lanes)
    row8 = row16[:V7X_SUBLANES]
    sums, keys = [], []
    for a in range(2):
        sums.append(s1[a:a + 1, :] + s2)
        keys.append(row16 + float(a * PEER_TOPK))
    s1_tail = jnp.where(row16 >= 2.0, s1, -jnp.inf)
    for b in range(2):
        sums.append(s1_tail + s2[b:b + 1, :])
        keys.append(row16 * float(PEER_TOPK) + float(b))
    for b in range(2, 5):
        sums.append(s1_tail[:V7X_SUBLANES] + s2[b:b + 1, :])
        keys.append(row8 * float(PEER_TOPK) + float(b))
    return jnp.concatenate(sums, axis=0), jnp.concatenate(keys, axis=0)


def _route_kernel(x_ref, wq_ref, keys_ref, i1_ref, i2_ref, g_ref,
                  sc_ref, tv_ref, tp_ref, sel1_ref, sel2_ref, gate_ref):
    tm = x_ref.shape[0]
    q = jnp.dot(x_ref[...].astype(BF16), wq_ref[...], preferred_element_type=F32)
    for hp in range(2 * PEER_HEADS):
        qhp = q[:, hp * PEER_HALF:(hp + 1) * PEER_HALF].astype(BF16)
        sc_ref[hp] = lax.dot_general(keys_ref[hp % 2], qhp, NT_DIMS, preferred_element_type=F32)

    def store(ref, slot):
        def f(r, val):
            ref[slot, r:r + 1, :] = val
        return f

    def head_body(h, carry):
        for c in range(tm // V7X_LANES):
            lanes = slice(c * V7X_LANES, (c + 1) * V7X_LANES)
            for p in range(2):
                _top16_by_row(sc_ref[2 * h + p, :, lanes], store(tv_ref, p), store(tp_ref, p))
            comb, comb_keys = _pair_candidates(tv_ref[0], tv_ref[1])
            _top16(comb, comb_keys, store(tv_ref, 2), store(tp_ref, 2))
            top_s = tv_ref[2]
            pos = tp_ref[2].astype(jnp.int32)
            a_sel = jnp.right_shift(pos, 4)
            b_sel = jnp.bitwise_and(pos, PEER_TOPK - 1)
            i1 = tp_ref[0]
            i2 = tp_ref[1]
            e1 = jnp.zeros_like(top_s)
            e2 = jnp.zeros_like(top_s)
            for a in range(PEER_TOPK):
                e1 = jnp.where(a_sel == a, i1[a:a + 1, :], e1)
                e2 = jnp.where(b_sel == a, i2[a:a + 1, :], e2)
            ex = jnp.exp(top_s - top_s[0:1, :])
            gate = ex / jnp.sum(ex, axis=0, keepdims=True)
            base = pl.multiple_of(h * PEER_TOPK, PEER_TOPK)
            sel1_ref[pl.ds(base, PEER_TOPK), lanes] = e1
            sel2_ref[pl.ds(base, PEER_TOPK), lanes] = e2
            gate_ref[pl.ds(base, PEER_TOPK), lanes] = gate
        return carry

    lax.fori_loop(0, PEER_HEADS, head_body, 0)
    for c in range(tm // V7X_LANES):
        lanes = slice(c * V7X_LANES, (c + 1) * V7X_LANES)
        rows = slice(c * V7X_LANES, (c + 1) * V7X_LANES)
        i1_ref[rows, :] = sel1_ref[:, lanes].T
        i2_ref[rows, :] = sel2_ref[:, lanes].T
        g_ref[rows, :] = gate_ref[:, lanes].T


def _route(x2d, wq, keys):
    n = x2d.shape[0]
    tm = min(ROUTE_ROWS, n)
    out = jax.ShapeDtypeStruct((n, PEER_SEL), F32)
    out_spec = pl.BlockSpec((tm, PEER_SEL), lambda i: (i, 0))
    est = 2 * (tm * D_MODEL * 4 + D_MODEL * PEER_QCOLS * 2) + 3 * tm * PEER_QCOLS * 4 + 8 * tm * PEER_SEL * 4
    return pl.pallas_call(
        _route_kernel,
        grid=(n // tm,),
        in_specs=[pl.BlockSpec((tm, D_MODEL), lambda i: (i, 0)),
                  pl.BlockSpec((D_MODEL, PEER_QCOLS), lambda i: (0, 0)),
                  pl.BlockSpec((2, PEER_NKEYS, PEER_HALF), lambda i: (0, 0, 0))],
        out_specs=[out_spec] * 3,
        out_shape=[out] * 3,
        scratch_shapes=[pltpu.VMEM((2 * PEER_HEADS, PEER_NKEYS, tm), F32),
                        pltpu.VMEM((3, PEER_TOPK, V7X_LANES), F32),
                        pltpu.VMEM((3, PEER_TOPK, V7X_LANES), F32),
                        pltpu.VMEM((PEER_SEL, tm), F32),
                        pltpu.VMEM((PEER_SEL, tm), F32),
                        pltpu.VMEM((PEER_SEL, tm), F32)],
        compiler_params=pltpu.CompilerParams(
            dimension_semantics=("parallel",), vmem_limit_bytes=_vmem_limit(est)),
        name="peer_route",
    )(x2d, wq, keys)


def _peer_kernel(x_ref, i1_ref, i2_ref, g_ref, ut_lo_ref, ut_hi_ref, v_lo_ref, v_hi_ref, lng_ref, lnb_ref, o_ref,
                 xb_ref, gate_ref, acc_ref, *, alpha):
    tm = x_ref.shape[0]
    j = pl.program_id(1)

    @pl.when(j == 0)
    def _():
        xb_ref[...] = x_ref[...].astype(BF16)
        acc_ref[...] = jnp.zeros_like(acc_ref)
        key_row = lax.broadcasted_iota(jnp.int32, (PEER_NKEYS, PEER_SEL), 0).astype(F32).astype(BF16)
        one = jnp.ones((PEER_NKEYS, PEER_SEL), BF16)
        zero = jnp.zeros((PEER_NKEYS, PEER_SEL), BF16)

        def token_body(t, carry):
            r1 = i1_ref[pl.ds(t, 1), :].astype(BF16)
            r2 = i2_ref[pl.ds(t, 1), :].astype(BF16)
            gg = jnp.broadcast_to(g_ref[pl.ds(t, 1), :].astype(BF16), (PEER_NKEYS, PEER_SEL))
            hot1 = jnp.where(key_row == r1, one, zero)
            wgt2 = jnp.where(key_row == r2, gg, zero)
            tile = lax.dot_general(hot1, wgt2, NT_DIMS, preferred_element_type=F32)
            packed = pltpu.bitcast(tile.astype(BF16), jnp.uint32)
            gate_ref[pl.ds(pl.multiple_of(t * GATE_PITCH, V7X_SUBLANES), PEER_KEY_HALF), :] = packed
            return carry

        lax.fori_loop(0, tm, token_body, 0, unroll=GATE_UNROLL)

    xb = xb_ref[...]
    h_lo = jnp.dot(xb, ut_lo_ref[...], preferred_element_type=F32)
    h_hi = jnp.dot(xb, ut_hi_ref[...], preferred_element_type=F32)
    parts_lo, parts_hi = [], []
    for r in range(PEER_GROUPS):
        word = gate_ref[pl.ds(j * PEER_GROUPS + r, tm, stride=GATE_PITCH), :]
        cols = slice(r * PEER_NKEYS, (r + 1) * PEER_NKEYS)
        gate_lo = lax.bitcast_convert_type(jnp.left_shift(word, jnp.uint32(16)), F32)
        gate_hi = lax.bitcast_convert_type(jnp.bitwise_and(word, jnp.uint32(0xFFFF0000)), F32)
        for h, gate, parts in ((h_lo, gate_lo, parts_lo), (h_hi, gate_hi, parts_hi)):
            hr = h[:, cols]
            gelu = 0.5 * hr * (1.0 + lax.erf(hr * math.sqrt(0.5)))
            parts.append((gelu * gate).astype(BF16))
    acc = jnp.dot(jnp.concatenate(parts_lo, axis=1), v_lo_ref[...], preferred_element_type=F32)
    acc = acc + jnp.dot(jnp.concatenate(parts_hi, axis=1), v_hi_ref[...], preferred_element_type=F32)
    acc_ref[...] += acc

    @pl.when(j == pl.num_programs(1) - 1)
    def _():
        o_ref[...] = _layer_norm(alpha * x_ref[...] + acc_ref[...], lng_ref[...], lnb_ref[...])


def _peer_mix(x2d, i1, i2, g, u_t, v_tab, ln_g, ln_b, alpha):
    n = x2d.shape[0]
    tm = min(PEER_ROWS, n)
    once = pl.Buffered(1)
    row_spec = lambda w: pl.BlockSpec((tm, w), lambda i, j: (i, 0), pipeline_mode=once)
    vec = pl.BlockSpec((1, D_MODEL), lambda i, j: (0, 0))
    gate_bytes = tm * GATE_PITCH * PEER_NKEYS * 4
    est = gate_bytes + 2 * tm * D_MODEL * 4 + 3 * tm * PEER_SEL * 4 + 8 * PEER_SLAB * D_MODEL * 2 \
        + tm * D_MODEL * 6 + 8 * tm * PEER_SLAB * 4
    return pl.pallas_call(
        functools.partial(_peer_kernel, alpha=alpha),
        grid=(n // tm, PEER_STEPS),
        in_specs=[row_spec(D_MODEL), row_spec(PEER_SEL), row_spec(PEER_SEL), row_spec(PEER_SEL),
                  pl.BlockSpec((D_MODEL, PEER_SLAB), lambda i, j: (0, j)),
                  pl.BlockSpec((D_MODEL, PEER_SLAB), lambda i, j: (0, j + PEER_STEPS)),
                  pl.BlockSpec((PEER_SLAB, D_MODEL), lambda i, j: (j, 0)),
                  pl.BlockSpec((PEER_SLAB, D_MODEL), lambda i, j: (j + PEER_STEPS, 0)),
                  vec, vec],
        out_specs=pl.BlockSpec((tm, D_MODEL), lambda i, j: (i, 0)),
        out_shape=jax.ShapeDtypeStruct((n, D_MODEL), F32),
        scratch_shapes=[pltpu.VMEM((tm, D_MODEL), BF16),
                        pltpu.VMEM((tm * GATE_PITCH, PEER_NKEYS), jnp.uint32),
                        pltpu.VMEM((tm, D_MODEL), F32)],
        compiler_params=pltpu.CompilerParams(
            dimension_semantics=("parallel", "arbitrary"), vmem_limit_bytes=_vmem_limit(est)),
        name="peer_mix_ln",
    )(x2d, i1, i2, g, u_t, u_t, v_tab, v_tab, ln_g, ln_b)


def _rope_tables(seq):
    inv_freq = ROPE_THETA ** (-jnp.arange(0, ATTN_QKDIM, 2, dtype=F32) / ATTN_QKDIM)
    ang = jnp.arange(seq, dtype=F32)[:, None] * inv_freq[None, :]
    ang = jnp.concatenate([ang, ang], axis=-1)
    sign = jnp.where(jnp.arange(ATTN_QKDIM) < ROPE_HALF, -1.0, 1.0).astype(F32)
    reps = ATTN_WIDTH // ATTN_QKDIM
    return jnp.tile(jnp.cos(ang), (1, reps)), jnp.tile(jnp.sin(ang) * sign, (1, reps))


def _trunk(x, layers, depth):
    b, s, _ = x.shape
    alpha = (2 * depth) ** 0.25
    cos, sin = _rope_tables(s)
    x2d = x.reshape(b * s, D_MODEL)
    for l, p in enumerate(layers):
        lam_init = 0.8 - 0.6 * math.exp(-0.3 * l)
        yc, q, k, vt = _inproj(x2d, p["w_in"], p["wv_t"], p["conv_w"], cos, sin, s)
        shp = (b, s, ATTN_WIDTH)
        ya = _attention(q.reshape(shp), k.reshape(shp), vt,
                        p["lq1"], p["lk1"], p["lq2"], p["lk2"], p["subln_g"], lam_init)
        x2d = _outproj(yc, ya.reshape(b * s, ATTN_WIDTH), x2d, p["w_out"], p["ln1_g"], p["ln1_b"], alpha)
        i1, i2, g = _route(x2d, p["wq"], p["keys"])
        x2d = _peer_mix(x2d, i1, i2, g, p["u_t"], p["v"], p["ln2_g"], p["ln2_b"], alpha)
    return x2d.reshape(b, s, D_MODEL)


def kernel(x_prompt, x_sample, w_in, w_out, conv_w, lam_q1, lam_k1, lam_q2, lam_k2, subln_g, ln1_g, ln1_b,
           peer_wq, peer_keys, peer_u, peer_v, ln2_g, ln2_b):
    depth = w_in.shape[0]
    v_cols = slice(IN_COLS - ATTN_WIDTH, IN_COLS)

    def pair_order(tab):
        d = tab.shape[1]
        return tab.reshape(PEER_KEY_HALF, 2, PEER_NKEYS, d).transpose(1, 0, 2, 3).reshape(PEER_N, d)

    layers = []
    for l in range(depth):
        row = lambda a: a[l][None, :]
        layers.append(dict(
            w_in=w_in[l].astype(BF16), wv_t=w_in[l][:, v_cols].T.astype(BF16),
            w_out=w_out[l].astype(BF16), conv_w=conv_w[l],
            lq1=row(lam_q1), lk1=row(lam_k1), lq2=row(lam_q2), lk2=row(lam_k2), subln_g=row(subln_g),
            ln1_g=row(ln1_g), ln1_b=row(ln1_b), ln2_g=row(ln2_g), ln2_b=row(ln2_b),
            wq=peer_wq[l].astype(BF16), keys=peer_keys[l].astype(BF16),
            u_t=pair_order(peer_u[l].astype(BF16)).T, v=pair_order(peer_v[l].astype(BF16))))
    return (_trunk(x_prompt, layers, depth), _trunk(x_sample, layers, depth))
```

```python
import functools
import math

import jax
import jax.numpy as jnp
from jax import lax
from jax.experimental import pallas as pl
from jax.experimental.pallas import tpu as pltpu

F32 = jnp.float32
BF16 = jnp.bfloat16

D_MODEL = 1024
CONV_WIDTH = 512
ATTN_WIDTH = 512
N_ATTN_HEADS = 4
ATTN_VDIM = 128
ATTN_QKDIM = 64
ROPE_HALF = ATTN_QKDIM // 2
IN_COLS = 3 * CONV_WIDTH + 3 * ATTN_WIDTH
ROPE_THETA = 10000.0
PEER_HEADS = 8
PEER_NKEYS = 128
PEER_N = PEER_NKEYS * PEER_NKEYS
PEER_HALF = 128
PEER_QCOLS = PEER_HEADS * 2 * PEER_HALF
PEER_TOPK = 16
PEER_SEL = PEER_HEADS * PEER_TOPK
LN_EPS = 1e-5

V7X_LANES = 128
V7X_SUBLANES = 8
V7X_VMEM_BYTES = 64 * 1024 * 1024
MIB = 1024 * 1024

INPROJ_ROWS = 512
ATTN_Q_ROWS = 256
ATTN_KEY_CHUNK = 512
ATTN_LOOKAHEAD = 6
OUTPROJ_ROWS = 512
ROUTE_ROWS = 256
PEER_ROWS = 512
PEER_SLAB = 1024
PEER_GROUPS = PEER_SLAB // PEER_NKEYS
PEER_KEY_HALF = PEER_NKEYS // 2
PEER_STEPS = PEER_KEY_HALF // PEER_GROUPS
GATE_PITCH = PEER_KEY_HALF + V7X_SUBLANES
GATE_UNROLL = 32

NT_DIMS = (((1,), (1,)), ((), ()))


def _vmem_limit(nbytes):
    return int(min(V7X_VMEM_BYTES - 6 * MIB, max(32 * MIB, nbytes)))


def _layer_norm(z, g, b):
    mu = jnp.mean(z, axis=-1, keepdims=True)
    zc = z - mu
    var = jnp.mean(zc * zc, axis=-1, keepdims=True)
    return zc * lax.rsqrt(var + LN_EPS) * g + b


def _inproj_kernel(x_ref, xp_ref, xn_ref, w_ref, wvt_ref, cw_ref, cos_ref, sin_ref,
                   yc_ref, q_ref, k_ref, vt_ref, *, blocks_per_seq):
    tm = x_ref.shape[0]
    pos = lax.rem(pl.program_id(0), blocks_per_seq)
    xe = jnp.concatenate([x_ref[...], xp_ref[...], xn_ref[...]], axis=0).astype(BF16)
    xb = xe[:tm]

    def proj(v, c0, c1):
        return jnp.dot(v, w_ref[:, c0:c1], preferred_element_type=F32)

    g_b = proj(xb, 0, CONV_WIDTH)
    ue = proj(xe, CONV_WIDTH, 2 * CONV_WIDTH) * proj(xe, 2 * CONV_WIDTH, 3 * CONV_WIDTH)
    u = ue[:tm]
    u_prev = jnp.where(pos == 0, 0.0, ue[tm + V7X_SUBLANES - 1:tm + V7X_SUBLANES])
    u_next = jnp.where(pos == blocks_per_seq - 1, 0.0, ue[tm + V7X_SUBLANES:tm + V7X_SUBLANES + 1])
    row = lax.broadcasted_iota(jnp.int32, u.shape, 0)
    u_m1 = jnp.where(row == 0, u_prev, pltpu.roll(u, 1, 0))
    u_p1 = jnp.where(row == tm - 1, u_next, pltpu.roll(u, tm - 1, 0))
    conv = u_m1 * cw_ref[0:1, :] + u * cw_ref[1:2, :] + u_p1 * cw_ref[2:3, :]
    yc_ref[...] = (g_b * conv).astype(yc_ref.dtype)

    cos = cos_ref[...]
    sin = sin_ref[...]
    lane = lax.broadcasted_iota(jnp.int32, cos.shape, 1)
    first_half = jnp.bitwise_and(lane, ATTN_QKDIM - 1) < ROPE_HALF

    def rope(t):
        n = t.shape[1]
        rot = jnp.where(first_half, pltpu.roll(t, n - ROPE_HALF, 1), pltpu.roll(t, ROPE_HALF, 1))
        return t * cos + rot * sin

    c = 3 * CONV_WIDTH
    scale = math.log2(math.e) / math.sqrt(ATTN_QKDIM)
    q_ref[...] = (rope(proj(xb, c, c + ATTN_WIDTH)) * scale).astype(q_ref.dtype)
    k_ref[...] = rope(proj(xb, c + ATTN_WIDTH, c + 2 * ATTN_WIDTH)).astype(k_ref.dtype)
    vt_ref[...] = lax.dot_general(wvt_ref[...], xb, NT_DIMS, preferred_element_type=F32).astype(vt_ref.dtype)


def _inproj(x2d, w_in, wv_t, conv_w, cos, sin, seq):
    n = x2d.shape[0]
    tm = min(INPROJ_ROWS, seq)
    bps = seq // tm
    nblk = n // tm
    r8 = tm // V7X_SUBLANES
    last8 = n // V7X_SUBLANES - 1
    row_spec = lambda w: pl.BlockSpec((tm, w), lambda i: (i, 0))
    tab_spec = pl.BlockSpec((tm, ATTN_WIDTH), lambda i: (lax.rem(i, bps), 0))
    out = jax.ShapeDtypeStruct((n, CONV_WIDTH), BF16)
    est = 2 * (tm * D_MODEL * 4 + D_MODEL * IN_COLS * 2 + 2 * tm * ATTN_WIDTH * 4 + 4 * tm * 512 * 2) \
        + 12 * tm * 512 * 4 + 2 * tm * D_MODEL * 4
    return pl.pallas_call(
        functools.partial(_inproj_kernel, blocks_per_seq=bps),
        grid=(nblk,),
        in_specs=[
            row_spec(D_MODEL),
            pl.BlockSpec((V7X_SUBLANES, D_MODEL), lambda i: (jnp.maximum(i * r8 - 1, 0), 0)),
            pl.BlockSpec((V7X_SUBLANES, D_MODEL), lambda i: (jnp.minimum((i + 1) * r8, last8), 0)),
            pl.BlockSpec((D_MODEL, IN_COLS), lambda i: (0, 0)),
            pl.BlockSpec((ATTN_WIDTH, D_MODEL), lambda i: (0, 0)),
            pl.BlockSpec((3, CONV_WIDTH), lambda i: (0, 0)),
            tab_spec, tab_spec,
        ],
        out_specs=[row_spec(CONV_WIDTH)] * 3 + [pl.BlockSpec((ATTN_WIDTH, tm), lambda i: (0, i))],
        out_shape=[out] * 3 + [jax.ShapeDtypeStruct((ATTN_WIDTH, n), BF16)],
        compiler_params=pltpu.CompilerParams(
            dimension_semantics=("parallel",), vmem_limit_bytes=_vmem_limit(est)),
        name="inproj_conv_rope",
    )(x2d, x2d, x2d, w_in, wv_t, conv_w, cos, sin)


def _attn_kernel(q_ref, k_ref, vt_ref, lq1_ref, lk1_ref, lq2_ref, lk2_ref, g_ref, o_ref, *, lam_init):
    q = q_ref[...]
    s = k_ref.shape[0]
    ck = min(ATTN_KEY_CHUNK, s)
    lane = lax.broadcasted_iota(jnp.int32, q.shape, 1)
    zero = jnp.zeros_like(q)
    q_maps = (jnp.where(lane < ATTN_QKDIM, q, zero), jnp.where(lane >= ATTN_QKDIM, q, zero))

    m_run, l_run, o_run = [None, None], [None, None], [None, None]
    ones_rows = jnp.ones((2 * V7X_SUBLANES, ck), BF16)
    items =[(c, mp) for c in range(s // ck) for mp in range(2)]

    def scores(item):
        c, mp = item
        return lax.dot_general(k_ref[c * ck:(c + 1) * ck, :], q_maps[mp], NT_DIMS,
                               preferred_element_type=F32)

    pending = [scores(it) for it in items[:ATTN_LOOKAHEAD]]
    for n, (c, mp) in enumerate(items):
        keys = slice(c * ck, (c + 1) * ck)
        if n + ATTN_LOOKAHEAD < len(items):
            pending.append(scores(items[n + ATTN_LOOKAHEAD]))
        st = pending.pop(0)
        m_c = jnp.max(st, axis=0, keepdims=True)
        m_new = m_c if c == 0 else jnp.maximum(m_run[mp], m_c)
        e = jnp.exp2((st - m_new).astype(BF16))
        pv_ext = jnp.dot(jnp.concatenate([vt_ref[:, keys], ones_rows], axis=0), e,
                         preferred_element_type=F32)
        pv = pv_ext[:ATTN_VDIM]
        l_c = pv_ext[ATTN_VDIM:ATTN_VDIM + 1]
        if c == 0:
            l_run[mp], o_run[mp] = l_c, pv
        else:
            rescale = jnp.exp2(m_run[mp] - m_new)
            l_run[mp] = l_run[mp] * rescale + l_c
            o_run[mp] = o_run[mp] * rescale + pv
        m_run[mp] = m_new

    lam = (jnp.exp(jnp.sum(lq1_ref[...] * lk1_ref[...], axis=-1, keepdims=True))
           - jnp.exp(jnp.sum(lq2_ref[...] * lk2_ref[...], axis=-1, keepdims=True)) + lam_init)
    ot = o_run[0] / l_run[0] - lam * (o_run[1] / l_run[1])
    ot = ot * lax.rsqrt(jnp.mean(ot * ot, axis=0, keepdims=True) + LN_EPS)
    o_ref[...] = (ot.T * g_ref[...] * (1.0 - lam_init)).astype(o_ref.dtype)


def _attention(q, k, vt, lq1, lk1, lq2, lk2, subln_g, lam_init):
    b, s, _ = q.shape
    tq = min(ATTN_Q_ROWS, s)
    q_spec = pl.BlockSpec((None, tq, ATTN_VDIM), lambda bi, h, i: (bi, i, h))
    k_spec = pl.BlockSpec((None, s, ATTN_VDIM), lambda bi, h, i: (bi, 0, h))
    vt_spec = pl.BlockSpec((ATTN_VDIM, s), lambda bi, h, i: (h, bi))
    vec = lambda w: pl.BlockSpec((1, w), lambda bi, h, i: (0, 0))
    est = 4 * s * ATTN_VDIM * 2 + 2 * tq * s * (4 + 4 + 2) + 8 * tq * ATTN_VDIM * 4
    return pl.pallas_call(
        functools.partial(_attn_kernel, lam_init=lam_init),
        grid=(b, N_ATTN_HEADS, s // tq),
        in_specs=[q_spec, k_spec, vt_spec, vec(ATTN_QKDIM), vec(ATTN_QKDIM), vec(ATTN_QKDIM),
                  vec(ATTN_QKDIM), vec(ATTN_VDIM)],
        out_specs=q_spec,
        out_shape=jax.ShapeDtypeStruct((b, s, ATTN_WIDTH), BF16),
        compiler_params=pltpu.CompilerParams(
            dimension_semantics=("parallel", "parallel", "parallel"), vmem_limit_bytes=_vmem_limit(est)),
        name="diff_attention",
    )(q, k, vt, lq1, lk1, lq2, lk2, subln_g)


def _outproj_kernel(yc_ref, ya_ref, x_ref, w_ref, g_ref, b_ref, o_ref, *, alpha):
    m = jnp.dot(yc_ref[...], w_ref[:CONV_WIDTH, :], preferred_element_type=F32)
    m = m + jnp.dot(ya_ref[...], w_ref[CONV_WIDTH:, :], preferred_element_type=F32)
    o_ref[...] = _layer_norm(alpha * x_ref[...] + m, g_ref[...], b_ref[...])


def _outproj(yc, ya, x2d, w_out, g, b, alpha):
    n = x2d.shape[0]
    tm = min(OUTPROJ_ROWS, n)
    row_spec = lambda w: pl.BlockSpec((tm, w), lambda i: (i, 0))
    vec = pl.BlockSpec((1, D_MODEL), lambda i: (0, 0))
    est = 2 * (2 * tm * 512 * 2 + 2 * tm * D_MODEL * 4 + D_MODEL * D_MODEL * 2) + 4 * tm * D_MODEL * 4
    return pl.pallas_call(
        functools.partial(_outproj_kernel, alpha=alpha),
        grid=(n // tm,),
        in_specs=[row_spec(CONV_WIDTH), row_spec(ATTN_WIDTH), row_spec(D_MODEL),
                  pl.BlockSpec((D_MODEL, D_MODEL), lambda i: (0, 0)), vec, vec],
        out_specs=row_spec(D_MODEL),
        out_shape=jax.ShapeDtypeStruct((n, D_MODEL), F32),
        compiler_params=pltpu.CompilerParams(
            dimension_semantics=("parallel",), vmem_limit_bytes=_vmem_limit(est)),
        name="outproj_ln",
    )(yc, ya, x2d, w_out, g, b)


def _row_index(n, lanes):
    return lax.broadcasted_iota(jnp.int32, (n, lanes), 0).astype(F32)


def _top16(s, keys, store_val, store_key):
    for r in range(PEER_TOPK):
        m = jnp.max(s, axis=0, keepdims=True)
        key = jnp.min(jnp.where(s == m, keys, jnp.inf), axis=0, keepdims=True)
        s = jnp.where(keys == key, -jnp.inf, s)
        store_val(r, m)
        store_key(r, key)


def _top16_by_row(s, store_val, store_key):
    n, lanes = s.shape
    nslab = n // V7X_SUBLANES
    row8 = _row_index(V7X_SUBLANES, lanes)
    slab_rows = [row8 + float(V7X_SUBLANES * i) for i in range(nslab)]
    rows = _row_index(n, lanes)
    for r in range(PEER_TOPK):
        vals = [s[i * V7X_SUBLANES:(i + 1) * V7X_SUBLANES, :] for i in range(nslab)]
        idxs = slab_rows
        while len(vals) > 1:
            nxt_v, nxt_i = [], []
            for a in range(0, len(vals), 2):
                left = vals[a] >= vals[a + 1]
                nxt_v.append(jnp.maximum(vals[a], vals[a + 1]))
                nxt_i.append(jnp.where(left, idxs[a], idxs[a + 1]))
            vals, idxs = nxt_v, nxt_i
        m = jnp.max(vals[0], axis=0, keepdims=True)
        key = jnp.min(jnp.where(vals[0] == m, idxs[0], jnp.inf), axis=0, keepdims=True)
        s = jnp.where(rows == key, -jnp.inf, s)
        store_val(r, m)
        store_key(r, key)


def _pair_candidates(s1, s2):
    lanes = s1.shape[1]
    row16 = _row_index(PEER_TOPK, lanes)
    row8 = row16[:V7X_SUBLANES]
    sums, keys = [], []
    for a in range(2):
        sums.append(s1[a:a + 1, :] + s2)
        keys.append(row16 + float(a * PEER_TOPK))
    s1_tail = jnp.where(row16 >= 2.0, s1, -jnp.inf)
    for b in range(2):
        sums.append(s1_tail + s2[b:b + 1, :])
        keys.append(row16 * float(PEER_TOPK) + float(b))
    for b in range(2, 5):
        sums.append(s1_tail[:V7X_SUBLANES] + s2[b:b + 1, :])
        keys.append(row8 * float(PEER_TOPK) + float(b))
    return jnp.concatenate(sums, axis=0), jnp.concatenate(keys, axis=0)


def _route_kernel(x_ref, wq_ref, keys_ref, i1_ref, i2_ref, g_ref,
                  sc_ref, tv_ref, tp_ref, sel1_ref, sel2_ref, gate_ref):
    tm = x_ref.shape[0]
    q = jnp.dot(x_ref[...].astype(BF16), wq_ref[...], preferred_element_type=F32)
    for hp in range(2 * PEER_HEADS):
        qhp = q[:, hp * PEER_HALF:(hp + 1) * PEER_HALF].astype(BF16)
        sc_ref[hp] = lax.dot_general(keys_ref[hp % 2], qhp, NT_DIMS, preferred_element_type=F32)

    def store(ref, slot):
        def f(r, val):
            ref[slot, r:r + 1, :] = val
        return f

    def head_body(h, carry):
        for c in range(tm // V7X_LANES):
            lanes = slice(c * V7X_LANES, (c + 1) * V7X_LANES)
            for p in range(2):
                _top16_by_row(sc_ref[2 * h + p, :, lanes], store(tv_ref, p), store(tp_ref, p))
            comb, comb_keys = _pair_candidates(tv_ref[0], tv_ref[1])
            _top16(comb, comb_keys, store(tv_ref, 2), store(tp_ref, 2))
            top_s = tv_ref[2]
            pos = tp_ref[2].astype(jnp.int32)
            a_sel = jnp.right_shift(pos, 4)
            b_sel = jnp.bitwise_and(pos, PEER_TOPK - 1)
            i1 = tp_ref[0]
            i2 = tp_ref[1]
            e1 = jnp.zeros_like(top_s)
            e2 = jnp.zeros_like(top_s)
            for a in range(PEER_TOPK):
                e1 = jnp.where(a_sel == a, i1[a:a + 1, :], e1)
                e2 = jnp.where(b_sel == a, i2[a:a + 1, :], e2)
            ex = jnp.exp(top_s - top_s[0:1, :])
            gate = ex / jnp.sum(ex, axis=0, keepdims=True)
            base = pl.multiple_of(h * PEER_TOPK, PEER_TOPK)
            sel1_ref[pl.ds(base, PEER_TOPK), lanes] = e1
            sel2_ref[pl.ds(base, PEER_TOPK), lanes] = e2
            gate_ref[pl.ds(base, PEER_TOPK), lanes] = gate
        return carry

    lax.fori_loop(0, PEER_HEADS, head_body, 0)
    for c in range(tm // V7X_LANES):
        lanes = slice(c * V7X_LANES, (c + 1) * V7X_LANES)
        rows = slice(c * V7X_LANES, (c + 1) * V7X_LANES)
        i1_ref[rows, :] = sel1_ref[:, lanes].T
        i2_ref[rows, :] = sel2_ref[:, lanes].T
        g_ref[rows, :] = gate_ref[:, lanes].T


def _route(x2d, wq, keys):
    n = x2d.shape[0]
    tm = min(ROUTE_ROWS, n)
    out = jax.ShapeDtypeStruct((n, PEER_SEL), F32)
    out_spec = pl.BlockSpec((tm, PEER_SEL), lambda i: (i, 0))
    est = 2 * (tm * D_MODEL * 4 + D_MODEL * PEER_QCOLS * 2) + 3 * tm * PEER_QCOLS * 4 + 8 * tm * PEER_SEL * 4
    return pl.pallas_call(
        _route_kernel,
        grid=(n // tm,),
        in_specs=[pl.BlockSpec((tm, D_MODEL), lambda i: (i, 0)),
                  pl.BlockSpec((D_MODEL, PEER_QCOLS), lambda i: (0, 0)),
                  pl.BlockSpec((2, PEER_NKEYS, PEER_HALF), lambda i: (0, 0, 0))],
        out_specs=[out_spec] * 3,
        out_shape=[out] * 3,
        scratch_shapes=[pltpu.VMEM((2 * PEER_HEADS, PEER_NKEYS, tm), F32),
                        pltpu.VMEM((3, PEER_TOPK, V7X_LANES), F32),
                        pltpu.VMEM((3, PEER_TOPK, V7X_LANES), F32),
                        pltpu.VMEM((PEER_SEL, tm), F32),
                        pltpu.VMEM((PEER_SEL, tm), F32),
                        pltpu.VMEM((PEER_SEL, tm), F32)],
        compiler_params=pltpu.CompilerParams(
            dimension_semantics=("parallel",), vmem_limit_bytes=_vmem_limit(est)),
        name="peer_route",
    )(x2d, wq, keys)


def _peer_kernel(x_ref, i1_ref, i2_ref, g_ref, ut_lo_ref, ut_hi_ref, v_lo_ref, v_hi_ref, lng_ref, lnb_ref, o_ref,
                 xb_ref, gate_ref, acc_ref, *, alpha):
    tm = x_ref.shape[0]
    j = pl.program_id(1)

    @pl.when(j == 0)
    def _():
        xb_ref[...] = x_ref[...].astype(BF16)
        acc_ref[...] = jnp.zeros_like(acc_ref)
        key_row = lax.broadcasted_iota(jnp.int32, (PEER_NKEYS, PEER_SEL), 0).astype(F32).astype(BF16)
        one = jnp.ones((PEER_NKEYS, PEER_SEL), BF16)
        zero = jnp.zeros((PEER_NKEYS, PEER_SEL), BF16)

        def token_body(t, carry):
            r1 = i1_ref[pl.ds(t, 1), :].astype(BF16)
            r2 = i2_ref[pl.ds(t, 1), :].astype(BF16)
            gg = jnp.broadcast_to(g_ref[pl.ds(t, 1), :].astype(BF16), (PEER_NKEYS, PEER_SEL))
            hot1 = jnp.where(key_row == r1, one, zero)
            wgt2 = jnp.where(key_row == r2, gg, zero)
            tile = lax.dot_general(hot1, wgt2, NT_DIMS, preferred_element_type=F32)
            packed = pltpu.bitcast(tile.astype(BF16), jnp.uint32)
            gate_ref[pl.ds(pl.multiple_of(t * GATE_PITCH, V7X_SUBLANES), PEER_KEY_HALF), :] = packed
            return carry

        lax.fori_loop(0, tm, token_body, 0, unroll=GATE_UNROLL)

    xb = xb_ref[...]
    h_lo = jnp.dot(xb, ut_lo_ref[...], preferred_element_type=F32)
    h_hi = jnp.dot(xb, ut_hi_ref[...], preferred_element_type=F32)
    parts_lo, parts_hi = [], []
    for r in range(PEER_GROUPS):
        word = gate_ref[pl.ds(j * PEER_GROUPS + r, tm, stride=GATE_PITCH), :]
        cols = slice(r * PEER_NKEYS, (r + 1) * PEER_NKEYS)
        gate_lo = lax.bitcast_convert_type(jnp.left_shift(word, jnp.uint32(16)), F32)
        gate_hi = lax.bitcast_convert_type(jnp.bitwise_and(word, jnp.uint32(0xFFFF0000)), F32)
        for h, gate, parts in ((h_lo, gate_lo, parts_lo), (h_hi, gate_hi, parts_hi)):
            hr = h[:, cols]
            gelu = 0.5 * hr * (1.0 + lax.erf(hr * math.sqrt(0.5)))
            parts.append((gelu * gate).astype(BF16))
    acc = jnp.dot(jnp.concatenate(parts_lo, axis=1), v_lo_ref[...], preferred_element_type=F32)
    acc = acc + jnp.dot(jnp.concatenate(parts_hi, axis=1), v_hi_ref[...], preferred_element_type=F32)
    acc_ref[...] += acc

    @pl.when(j == pl.num_programs(1) - 1)
    def _():
        o_ref[...] = _layer_norm(alpha * x_ref[...] + acc_ref[...], lng_ref[...], lnb_ref[...])


def _peer_mix(x2d, i1, i2, g, u_t, v_tab, ln_g, ln_b, alpha):
    n = x2d.shape[0]
    tm = min(PEER_ROWS, n)
    once = pl.Buffered(1)
    row_spec = lambda w: pl.BlockSpec((tm, w), lambda i, j: (i, 0), pipeline_mode=once)
    vec = pl.BlockSpec((1, D_MODEL), lambda i, j: (0, 0))
    gate_bytes = tm * GATE_PITCH * PEER_NKEYS * 4
    est = gate_bytes + 2 * tm * D_MODEL * 4 + 3 * tm * PEER_SEL * 4 + 8 * PEER_SLAB * D_MODEL * 2 \
        + tm * D_MODEL * 6 + 8 * tm * PEER_SLAB * 4
    return pl.pallas_call(
        functools.partial(_peer_kernel, alpha=alpha),
        grid=(n // tm, PEER_STEPS),
        in_specs=[row_spec(D_MODEL), row_spec(PEER_SEL), row_spec(PEER_SEL), row_spec(PEER_SEL),
                  pl.BlockSpec((None, D_MODEL, PEER_SLAB), lambda i, j: (j, 0, 0)),
                  pl.BlockSpec((None, D_MODEL, PEER_SLAB), lambda i, j: (j + PEER_STEPS, 0, 0)),
                  pl.BlockSpec((PEER_SLAB, D_MODEL), lambda i, j: (j, 0)),
                  pl.BlockSpec((PEER_SLAB, D_MODEL), lambda i, j: (j + PEER_STEPS, 0)),
                  vec, vec],
        out_specs=pl.BlockSpec((tm, D_MODEL), lambda i, j: (i, 0)),
        out_shape=jax.ShapeDtypeStruct((n, D_MODEL), F32),
        scratch_shapes=[pltpu.VMEM((tm, D_MODEL), BF16),
                        pltpu.VMEM((tm * GATE_PITCH, PEER_NKEYS), jnp.uint32),
                        pltpu.VMEM((tm, D_MODEL), F32)],
        compiler_params=pltpu.CompilerParams(
            dimension_semantics=("parallel", "arbitrary"), vmem_limit_bytes=_vmem_limit(est)),
        name="peer_mix_ln",
    )(x2d, i1, i2, g, u_t, u_t, v_tab, v_tab, ln_g, ln_b)


def _rope_tables(seq):
    inv_freq = ROPE_THETA ** (-jnp.arange(0, ATTN_QKDIM, 2, dtype=F32) / ATTN_QKDIM)
    ang = jnp.arange(seq, dtype=F32)[:, None] * inv_freq[None, :]
    ang = jnp.concatenate([ang, ang], axis=-1)
    sign = jnp.where(jnp.arange(ATTN_QKDIM) < ROPE_HALF, -1.0, 1.0).astype(F32)
    reps = ATTN_WIDTH // ATTN_QKDIM
    return jnp.tile(jnp.cos(ang), (1, reps)), jnp.tile(jnp.sin(ang) * sign, (1, reps))


def _trunk(x, layers, depth):
    b, s, _ = x.shape
    alpha = (2 * depth) ** 0.25
    cos, sin = _rope_tables(s)
    x2d = x.reshape(b * s, D_MODEL)
    for l, p in enumerate(layers):
        lam_init = 0.8 - 0.6 * math.exp(-0.3 * l)
        yc, q, k, vt = _inproj(x2d, p["w_in"], p["wv_t"], p["conv_w"], cos, sin, s)
        shp = (b, s, ATTN_WIDTH)
        ya = _attention(q.reshape(shp), k.reshape(shp), vt,
                        p["lq1"], p["lk1"], p["lq2"], p["lk2"], p["subln_g"], lam_init)
        x2d = _outproj(yc, ya.reshape(b * s, ATTN_WIDTH), x2d, p["w_out"], p["ln1_g"], p["ln1_b"], alpha)
        i1, i2, g = _route(x2d, p["wq"], p["keys"])
        x2d = _peer_mix(x2d, i1, i2, g, p["u_t"], p["v"], p["ln2_g"], p["ln2_b"], alpha)
    return x2d.reshape(b, s, D_MODEL)


def kernel(x_prompt, x_sample, w_in, w_out, conv_w, lam_q1, lam_k1, lam_q2, lam_k2, subln_g, ln1_g, ln1_b,
           peer_wq, peer_keys, peer_u, peer_v, ln2_g, ln2_b):
    depth = w_in.shape[0]
    v_cols = slice(IN_COLS - ATTN_WIDTH, IN_COLS)

    def pair_order(tab):
        d = tab.shape[1]
        return tab.reshape(PEER_KEY_HALF, 2, PEER_NKEYS, d).transpose(1, 0, 2, 3).reshape(PEER_N, d)

    layers = []
    for l in range(depth):
        row = lambda a: a[l][None, :]
        layers.append(dict(
            w_in=w_in[l].astype(BF16), wv_t=w_in[l][:, v_cols].T.astype(BF16),
            w_out=w_out[l].astype(BF16), conv_w=conv_w[l],
            lq1=row(lam_q1), lk1=row(lam_k1), lq2=row(lam_q2), lk2=row(lam_k2), subln_g=row(subln_g),
            ln1_g=row(ln1_g), ln1_b=row(ln1_b), ln2_g=row(ln2_g), ln2_b=row(ln2_b),
            wq=peer_wq[l].astype(BF16), keys=peer_keys[l].astype(BF16),
            u_t=pair_order(peer_u[l].astype(BF16)).reshape(PEER_N // PEER_SLAB, PEER_SLAB, D_MODEL).transpose(0, 2, 1),
            v=pair_order(peer_v[l].astype(BF16))))
    return (_trunk(x_prompt, layers, depth), _trunk(x_sample, layers, depth))
```

```python
import functools
import math

import jax
import jax.numpy as jnp
from jax import lax
from jax.experimental import pallas as pl
from jax.experimental.pallas import tpu as pltpu

F32 = jnp.float32
BF16 = jnp.bfloat16

D_MODEL = 1024
CONV_WIDTH = 512
ATTN_WIDTH = 512
N_ATTN_HEADS = 4
ATTN_VDIM = 128
ATTN_QKDIM = 64
ROPE_HALF = ATTN_QKDIM // 2
IN_COLS = 3 * CONV_WIDTH + 3 * ATTN_WIDTH
ROPE_THETA = 10000.0
PEER_HEADS = 8
PEER_NKEYS = 128
PEER_N = PEER_NKEYS * PEER_NKEYS
PEER_HALF = 128
PEER_QCOLS = PEER_HEADS * 2 * PEER_HALF
PEER_TOPK = 16
PEER_SEL = PEER_HEADS * PEER_TOPK
LN_EPS = 1e-5

V7X_LANES = 128
V7X_SUBLANES = 8
V7X_VMEM_BYTES = 64 * 1024 * 1024
MIB = 1024 * 1024

INPROJ_ROWS = 512
ATTN_Q_ROWS = 256
ATTN_KEY_CHUNK = 512
ATTN_LOOKAHEAD = 6
OUTPROJ_ROWS = 512
PEER_ROWS = 512
PEER_SLAB = 512
PEER_GROUPS = PEER_SLAB // PEER_NKEYS
PEER_KEY_HALF = PEER_NKEYS // 2
PEER_STEPS = PEER_KEY_HALF // PEER_GROUPS
GATE_PITCH = PEER_KEY_HALF + V7X_SUBLANES
GATE_UNROLL = 32

NT_DIMS = (((1,), (1,)), ((), ()))


def _vmem_limit(nbytes):
    return int(min(V7X_VMEM_BYTES - 6 * MIB, max(32 * MIB, nbytes)))


def _layer_norm(z, g, b):
    mu = jnp.mean(z, axis=-1, keepdims=True)
    zc = z - mu
    var = jnp.mean(zc * zc, axis=-1, keepdims=True)
    return zc * lax.rsqrt(var + LN_EPS) * g + b


def _inproj_kernel(x_ref, xp_ref, xn_ref, w_ref, wvt_ref, cw_ref, cos_ref, sin_ref,
                   yc_ref, q_ref, k_ref, vt_ref, *, blocks_per_seq):
    tm = x_ref.shape[0]
    pos = lax.rem(pl.program_id(0), blocks_per_seq)
    xe = jnp.concatenate([x_ref[...], xp_ref[...], xn_ref[...]], axis=0).astype(BF16)
    xb = xe[:tm]

    def proj(v, c0, c1):
        return jnp.dot(v, w_ref[:, c0:c1], preferred_element_type=F32)

    g_b = proj(xb, 0, CONV_WIDTH)
    ue = proj(xe, CONV_WIDTH, 2 * CONV_WIDTH) * proj(xe, 2 * CONV_WIDTH, 3 * CONV_WIDTH)
    u = ue[:tm]
    u_prev = jnp.where(pos == 0, 0.0, ue[tm + V7X_SUBLANES - 1:tm + V7X_SUBLANES])
    u_next = jnp.where(pos == blocks_per_seq - 1, 0.0, ue[tm + V7X_SUBLANES:tm + V7X_SUBLANES + 1])
    row = lax.broadcasted_iota(jnp.int32, u.shape, 0)
    u_m1 = jnp.where(row == 0, u_prev, pltpu.roll(u, 1, 0))
    u_p1 = jnp.where(row == tm - 1, u_next, pltpu.roll(u, tm - 1, 0))
    conv = u_m1 * cw_ref[0:1, :] + u * cw_ref[1:2, :] + u_p1 * cw_ref[2:3, :]
    yc_ref[...] = (g_b * conv).astype(yc_ref.dtype)

    cos = cos_ref[...]
    sin = sin_ref[...]
    lane = lax.broadcasted_iota(jnp.int32, cos.shape, 1)
    first_half = jnp.bitwise_and(lane, ATTN_QKDIM - 1) < ROPE_HALF

    def rope(t):
        n = t.shape[1]
        rot = jnp.where(first_half, pltpu.roll(t, n - ROPE_HALF, 1), pltpu.roll(t, ROPE_HALF, 1))
        return t * cos + rot * sin

    c = 3 * CONV_WIDTH
    scale = math.log2(math.e) / math.sqrt(ATTN_QKDIM)
    q_ref[...] = (rope(proj(xb, c, c + ATTN_WIDTH)) * scale).astype(q_ref.dtype)
    k_ref[...] = rope(proj(xb, c + ATTN_WIDTH, c + 2 * ATTN_WIDTH)).astype(k_ref.dtype)
    vt_ref[...] = lax.dot_general(wvt_ref[...], xb, NT_DIMS, preferred_element_type=F32).astype(vt_ref.dtype)


def _inproj(x2d, w_in, wv_t, conv_w, cos, sin, seq):
    n = x2d.shape[0]
    tm = min(INPROJ_ROWS, seq)
    bps = seq // tm
    nblk = n // tm
    r8 = tm // V7X_SUBLANES
    last8 = n // V7X_SUBLANES - 1
    row_spec = lambda w: pl.BlockSpec((tm, w), lambda i: (i, 0))
    tab_spec = pl.BlockSpec((tm, ATTN_WIDTH), lambda i: (lax.rem(i, bps), 0))
    out = jax.ShapeDtypeStruct((n, CONV_WIDTH), BF16)
    est = 2 * (tm * D_MODEL * 4 + D_MODEL * IN_COLS * 2 + 2 * tm * ATTN_WIDTH * 4 + 4 * tm * 512 * 2) \
        + 12 * tm * 512 * 4 + 2 * tm * D_MODEL * 4
    return pl.pallas_call(
        functools.partial(_inproj_kernel, blocks_per_seq=bps),
        grid=(nblk,),
        in_specs=[
            row_spec(D_MODEL),
            pl.BlockSpec((V7X_SUBLANES, D_MODEL), lambda i: (jnp.maximum(i * r8 - 1, 0), 0)),
            pl.BlockSpec((V7X_SUBLANES, D_MODEL), lambda i: (jnp.minimum((i + 1) * r8, last8), 0)),
            pl.BlockSpec((D_MODEL, IN_COLS), lambda i: (0, 0)),
            pl.BlockSpec((ATTN_WIDTH, D_MODEL), lambda i: (0, 0)),
            pl.BlockSpec((3, CONV_WIDTH), lambda i: (0, 0)),
            tab_spec, tab_spec,
        ],
        out_specs=[row_spec(CONV_WIDTH)] * 3 + [pl.BlockSpec((ATTN_WIDTH, tm), lambda i: (0, i))],
        out_shape=[out] * 3 + [jax.ShapeDtypeStruct((ATTN_WIDTH, n), BF16)],
        compiler_params=pltpu.CompilerParams(
            dimension_semantics=("parallel",), vmem_limit_bytes=_vmem_limit(est)),
        name="inproj_conv_rope",
    )(x2d, x2d, x2d, w_in, wv_t, conv_w, cos, sin)


def _attn_kernel(q_ref, k_ref, vt_ref, lq1_ref, lk1_ref, lq2_ref, lk2_ref, g_ref, o_ref, *, lam_init):
    q = q_ref[...]
    s = k_ref.shape[0]
    ck = min(ATTN_KEY_CHUNK, s)
    lane = lax.broadcasted_iota(jnp.int32, q.shape, 1)
    zero = jnp.zeros_like(q)
    q_maps = (jnp.where(lane < ATTN_QKDIM, q, zero), jnp.where(lane >= ATTN_QKDIM, q, zero))

    m_run, l_run, o_run = [None, None], [None, None], [None, None]
    items =[(c, mp) for c in range(s // ck) for mp in range(2)]

    def scores(item):
        c, mp = item
        return lax.dot_general(k_ref[c * ck:(c + 1) * ck, :], q_maps[mp], NT_DIMS,
                               preferred_element_type=F32)

    pending = [scores(it) for it in items[:ATTN_LOOKAHEAD]]
    for n, (c, mp) in enumerate(items):
        keys = slice(c * ck, (c + 1) * ck)
        if n + ATTN_LOOKAHEAD < len(items):
            pending.append(scores(items[n + ATTN_LOOKAHEAD]))
        st = pending.pop(0)
        m_c = jnp.max(st, axis=0, keepdims=True)
        m_new = m_c if c == 0 else jnp.maximum(m_run[mp], m_c)
        e = jnp.exp2(st - m_new)
        l_c = jnp.sum(e, axis=0, keepdims=True)
        pv = jnp.dot(vt_ref[:, keys], e.astype(vt_ref.dtype), preferred_element_type=F32)
        if c == 0:
            l_run[mp], o_run[mp] = l_c, pv
        else:
            rescale = jnp.exp2(m_run[mp] - m_new)
            l_run[mp] = l_run[mp] * rescale + l_c
            o_run[mp] = o_run[mp] * rescale + pv
        m_run[mp] = m_new

    lam = (jnp.exp(jnp.sum(lq1_ref[...] * lk1_ref[...], axis=-1, keepdims=True))
           - jnp.exp(jnp.sum(lq2_ref[...] * lk2_ref[...], axis=-1, keepdims=True)) + lam_init)
    ot = o_run[0] / l_run[0] - lam * (o_run[1] / l_run[1])
    ot = ot * lax.rsqrt(jnp.mean(ot * ot, axis=0, keepdims=True) + LN_EPS)
    o_ref[...] = (ot.T * g_ref[...] * (1.0 - lam_init)).astype(o_ref.dtype)


def _attention(q, k, vt, lq1, lk1, lq2, lk2, subln_g, lam_init):
    b, s, _ = q.shape
    tq = min(ATTN_Q_ROWS, s)
    q_spec = pl.BlockSpec((None, tq, ATTN_VDIM), lambda bi, h, i: (bi, i, h))
    k_spec = pl.BlockSpec((None, s, ATTN_VDIM), lambda bi, h, i: (bi, 0, h))
    vt_spec = pl.BlockSpec((ATTN_VDIM, s), lambda bi, h, i: (h, bi))
    vec = lambda w: pl.BlockSpec((1, w), lambda bi, h, i: (0, 0))
    est = 4 * s * ATTN_VDIM * 2 + 2 * tq * s * (4 + 4 + 2) + 8 * tq * ATTN_VDIM * 4
    return pl.pallas_call(
        functools.partial(_attn_kernel, lam_init=lam_init),
        grid=(b, N_ATTN_HEADS, s // tq),
        in_specs=[q_spec, k_spec, vt_spec, vec(ATTN_QKDIM), vec(ATTN_QKDIM), vec(ATTN_QKDIM),
                  vec(ATTN_QKDIM), vec(ATTN_VDIM)],
        out_specs=q_spec,
        out_shape=jax.ShapeDtypeStruct((b, s, ATTN_WIDTH), BF16),
        compiler_params=pltpu.CompilerParams(
            dimension_semantics=("parallel", "parallel", "parallel"), vmem_limit_bytes=_vmem_limit(est)),
        name="diff_attention",
    )(q, k, vt, lq1, lk1, lq2, lk2, subln_g)


def _outproj_kernel(yc_ref, ya_ref, x_ref, w_ref, g_ref, b_ref, o_ref, *, alpha):
    m = jnp.dot(yc_ref[...], w_ref[:CONV_WIDTH, :], preferred_element_type=F32)
    m = m + jnp.dot(ya_ref[...], w_ref[CONV_WIDTH:, :], preferred_element_type=F32)
    o_ref[...] = _layer_norm(alpha * x_ref[...] + m, g_ref[...], b_ref[...])


def _outproj(yc, ya, x2d, w_out, g, b, alpha):
    n = x2d.shape[0]
    tm = min(OUTPROJ_ROWS, n)
    row_spec = lambda w: pl.BlockSpec((tm, w), lambda i: (i, 0))
    vec = pl.BlockSpec((1, D_MODEL), lambda i: (0, 0))
    est = 2 * (2 * tm * 512 * 2 + 2 * tm * D_MODEL * 4 + D_MODEL * D_MODEL * 2) + 4 * tm * D_MODEL * 4
    return pl.pallas_call(
        functools.partial(_outproj_kernel, alpha=alpha),
        grid=(n // tm,),
        in_specs=[row_spec(CONV_WIDTH), row_spec(ATTN_WIDTH), row_spec(D_MODEL),
                  pl.BlockSpec((D_MODEL, D_MODEL), lambda i: (0, 0)), vec, vec],
        out_specs=row_spec(D_MODEL),
        out_shape=jax.ShapeDtypeStruct((n, D_MODEL), F32),
        compiler_params=pltpu.CompilerParams(
            dimension_semantics=("parallel",), vmem_limit_bytes=_vmem_limit(est)),
        name="outproj_ln",
    )(yc, ya, x2d, w_out, g, b)


def _row_index(n, lanes):
    return lax.broadcasted_iota(jnp.int32, (n, lanes), 0).astype(F32)


def _top16(s, keys, store_val, store_key):
    for r in range(PEER_TOPK):
        m = jnp.max(s, axis=0, keepdims=True)
        key = jnp.min(jnp.where(s == m, keys, jnp.inf), axis=0, keepdims=True)
        s = jnp.where(keys == key, -jnp.inf, s)
        store_val(r, m)
        store_key(r, key)


def _top16_by_row(s, store_val, store_key):
    n, lanes = s.shape
    nslab = n // V7X_SUBLANES
    row8 = _row_index(V7X_SUBLANES, lanes)
    slab_rows = [row8 + float(V7X_SUBLANES * i) for i in range(nslab)]
    rows = _row_index(n, lanes)
    for r in range(PEER_TOPK):
        vals = [s[i * V7X_SUBLANES:(i + 1) * V7X_SUBLANES, :] for i in range(nslab)]
        idxs = slab_rows
        while len(vals) > 1:
            nxt_v, nxt_i = [], []
            for a in range(0, len(vals), 2):
                left = vals[a] >= vals[a + 1]
                nxt_v.append(jnp.maximum(vals[a], vals[a + 1]))
                nxt_i.append(jnp.where(left, idxs[a], idxs[a + 1]))
            vals, idxs = nxt_v, nxt_i
        m = jnp.max(vals[0], axis=0, keepdims=True)
        key = jnp.min(jnp.where(vals[0] == m, idxs[0], jnp.inf), axis=0, keepdims=True)
        s = jnp.where(rows == key, -jnp.inf, s)
        store_val(r, m)
        store_key(r, key)
    return m


def _zero_after(x):
    bits = lax.bitcast_convert_type(x, jnp.uint32)
    return jnp.right_shift(jnp.right_shift(bits, jnp.uint32(16)), jnp.uint32(16)).astype(F32)


def _pair_candidates(s1, s2):
    lanes = s1.shape[1]
    row16 = _row_index(PEER_TOPK, lanes)
    row8 = row16[:V7X_SUBLANES]
    sums, keys = [], []
    for a in range(2):
        sums.append(s1[a:a + 1, :] + s2)
        keys.append(row16 + float(a * PEER_TOPK))
    s1_tail = jnp.where(row16 >= 2.0, s1, -jnp.inf)
    for b in range(2):
        sums.append(s1_tail + s2[b:b + 1, :])
        keys.append(row16 * float(PEER_TOPK) + float(b))
    for b in range(2, 5):
        sums.append(s1_tail[:V7X_SUBLANES] + s2[b:b + 1, :])
        keys.append(row8 * float(PEER_TOPK) + float(b))
    return jnp.concatenate(sums, axis=0), jnp.concatenate(keys, axis=0)


def _route_chunk(sc_ref, lanes, tv_ref, tp_ref, slot, after):
    def store(ref, which):
        def f(r, val):
            ref[slot, which, r:r + 1, :] = val
        return f

    last = _top16_by_row(sc_ref[0, :, lanes], store(tv_ref, 0), store(tp_ref, 0))
    _top16_by_row(sc_ref[1, :, lanes] + _zero_after(last), store(tv_ref, 1), store(tp_ref, 1))
    comb, comb_keys = _pair_candidates(tv_ref[slot, 0], tv_ref[slot, 1])
    if after is not None:
        comb = comb + after
    _top16(comb, comb_keys, store(tv_ref, 2), store(tp_ref, 2))
    top_s = tv_ref[slot, 2]
    pos = tp_ref[slot, 2].astype(jnp.int32)
    a_sel = jnp.right_shift(pos, 4)
    b_sel = jnp.bitwise_and(pos, PEER_TOPK - 1)
    i1 = tp_ref[slot, 0]
    i2 = tp_ref[slot, 1]
    e1 = jnp.zeros_like(top_s)
    e2 = jnp.zeros_like(top_s)
    for a in range(PEER_TOPK):
        e1 = jnp.where(a_sel == a, i1[a:a + 1, :], e1)
        e2 = jnp.where(b_sel == a, i2[a:a + 1, :], e2)
    ex = jnp.exp(top_s - top_s[0:1, :])
    return e1, e2, ex / jnp.sum(ex, axis=0, keepdims=True)


def _subkey_scores(xb_ref, part, tps, wq_ref, keys_ref, out_ref):
    x_part = xb_ref[pl.ds(pl.multiple_of(part * tps, tps), tps), :]
    q = jnp.dot(x_part, wq_ref[...], preferred_element_type=F32)
    for p in range(2):
        out_ref[p] = lax.dot_general(keys_ref[p], q[:, p * PEER_HALF:(p + 1) * PEER_HALF].astype(BF16), NT_DIMS,
                                     preferred_element_type=F32)


def _peer_kernel(x_ref, xn_ref, wq0_ref, wqn_ref, keys_ref, ut_lo_ref, ut_hi_ref, v_lo_ref, v_hi_ref,
                 lng_ref, lnb_ref, o_ref,
                 xb_ref, xnb_ref, gate_ref, acc_ref, sc_ref, tv_ref, tp_ref, sel1_ref, sel2_ref, selg_ref,
                 tok1_ref, tok2_ref, tokg_ref, *, alpha):
    tm = x_ref.shape[0]
    i = pl.program_id(0)
    j = pl.program_id(1)
    parts = sel1_ref.shape[0]
    tps = sel1_ref.shape[2]

    @pl.when(j == 0)
    def _():
        xb_ref[...] = x_ref[...].astype(BF16)
        xnb_ref[...] = xn_ref[...].astype(BF16)
        acc_ref[...] = jnp.zeros_like(acc_ref)
        _subkey_scores(xnb_ref, 0, tps, wq0_ref, keys_ref, sc_ref.at[0])

        @pl.when(i == 0)
        def _():
            gate_ref[...] = jnp.zeros_like(gate_ref)

        @pl.when(i > 0)
        def _():
            for part in range(parts):
                for c in range(tps // V7X_LANES):
                    lanes = slice(c * V7X_LANES, (c + 1) * V7X_LANES)
                    rows = slice(part * tps + c * V7X_LANES, part * tps + (c + 1) * V7X_LANES)
                    tok1_ref[rows, :] = sel1_ref[part, :, lanes].T
                    tok2_ref[rows, :] = sel2_ref[part, :, lanes].T
                    tokg_ref[rows, :] = selg_ref[part, :, lanes].T
            key_row = lax.broadcasted_iota(jnp.int32, (PEER_NKEYS, PEER_SEL), 0).astype(F32).astype(BF16)
            one = jnp.ones((PEER_NKEYS, PEER_SEL), BF16)
            zero = jnp.zeros((PEER_NKEYS, PEER_SEL), BF16)

            def token_body(t, carry):
                r1 = tok1_ref[pl.ds(t, 1), :].astype(BF16)
                r2 = tok2_ref[pl.ds(t, 1), :].astype(BF16)
                gg = jnp.broadcast_to(tokg_ref[pl.ds(t, 1), :].astype(BF16), (PEER_NKEYS, PEER_SEL))
                hot1 = jnp.where(key_row == r1, one, zero)
                wgt2 = jnp.where(key_row == r2, gg, zero)
                tile = lax.dot_general(hot1, wgt2, NT_DIMS, preferred_element_type=F32)
                packed = pltpu.bitcast(tile.astype(BF16), jnp.uint32)
                gate_ref[pl.ds(pl.multiple_of(t * GATE_PITCH, V7X_SUBLANES), PEER_KEY_HALF), :] = packed
                return carry

            lax.fori_loop(0, tm, token_body, 0, unroll=GATE_UNROLL)

    head = j // parts
    part = j % parts
    sc_cur = sc_ref.at[j % 2]
    base = pl.multiple_of(head * PEER_TOPK, PEER_TOPK)
    after = None
    for c in range(tps // V7X_LANES):
        lanes = slice(c * V7X_LANES, (c + 1) * V7X_LANES)
        e1, e2, gate = _route_chunk(sc_cur, lanes, tv_ref, tp_ref, c, after)
        sel1_ref[part, pl.ds(base, PEER_TOPK), lanes] = e1
        sel2_ref[part, pl.ds(base, PEER_TOPK), lanes] = e2
        selg_ref[part, pl.ds(base, PEER_TOPK), lanes] = gate
        after = _zero_after(e1[0:1, :] + e2[0:1, :] + gate[0:1, :])

    xb = xb_ref[...]
    h_lo = jnp.dot(xb, ut_lo_ref[...], preferred_element_type=F32)
    h_hi = jnp.dot(xb, ut_hi_ref[...], preferred_element_type=F32)
    parts_lo, parts_hi = [], []
    for r in range(PEER_GROUPS):
        word = gate_ref[pl.ds(j * PEER_GROUPS + r, tm, stride=GATE_PITCH), :]
        cols = slice(r * PEER_NKEYS, (r + 1) * PEER_NKEYS)
        gate_lo = lax.bitcast_convert_type(jnp.left_shift(word, jnp.uint32(16)), F32)
        gate_hi = lax.bitcast_convert_type(jnp.bitwise_and(word, jnp.uint32(0xFFFF0000)), F32)
        for h, gate, plist in ((h_lo, gate_lo, parts_lo), (h_hi, gate_hi, parts_hi)):
            hr = h[:, cols]
            gelu = 0.5 * hr * (1.0 + lax.erf(hr * math.sqrt(0.5)))
            plist.append((gelu * gate).astype(BF16))
    acc = jnp.dot(jnp.concatenate(parts_lo, axis=1), v_lo_ref[...], preferred_element_type=F32)
    acc = acc + jnp.dot(jnp.concatenate(parts_hi, axis=1), v_hi_ref[...], preferred_element_type=F32)
    acc_ref[...] += acc

    nxt = jnp.minimum(j + 1, pl.num_programs(1) - 1)
    _subkey_scores(xnb_ref, nxt % parts, tps, wqn_ref, keys_ref, sc_ref.at[(j + 1) % 2])

    @pl.when(j == pl.num_programs(1) - 1)
    def _():
        o_ref[...] = _layer_norm(alpha * x_ref[...] + acc_ref[...], lng_ref[...], lnb_ref[...])


def _peer(x2d, wq, keys, u_t, v_tab, ln_g, ln_b, alpha):
    n = x2d.shape[0]
    tm = min(PEER_ROWS, n)
    nb = n // tm
    assert PEER_STEPS % PEER_HEADS == 0
    parts = PEER_STEPS // PEER_HEADS
    tps = tm // parts
    assert tps % V7X_LANES == 0
    once = pl.Buffered(1)
    vec = pl.BlockSpec((1, D_MODEL), lambda i, j: (0, 0))
    cur = lambda i, j: (jnp.maximum(i - 1, 0), 0)
    gate_bytes = tm * GATE_PITCH * PEER_NKEYS * 4
    est = gate_bytes + 3 * tm * D_MODEL * 4 + 8 * PEER_SLAB * D_MODEL * 2 + tm * D_MODEL * 8 \
        + 8 * tm * PEER_SLAB * 4 + 16 * tm * PEER_SEL * 4
    sel = pltpu.VMEM((parts, PEER_SEL, tps), F32)
    tok = pltpu.VMEM((tm, PEER_SEL), F32)
    return pl.pallas_call(
        functools.partial(_peer_kernel, alpha=alpha),
        grid=(nb + 1, PEER_STEPS),
        in_specs=[pl.BlockSpec((tm, D_MODEL), cur, pipeline_mode=once),
                  pl.BlockSpec((tm, D_MODEL), lambda i, j: (jnp.minimum(i, nb - 1), 0), pipeline_mode=once),
                  pl.BlockSpec((D_MODEL, 2 * PEER_HALF), lambda i, j: (0, 0)),
                  pl.BlockSpec((D_MODEL, 2 * PEER_HALF),
                               lambda i, j: (0, jnp.minimum(j + 1, PEER_STEPS - 1) // parts)),
                  pl.BlockSpec((2, PEER_NKEYS, PEER_HALF), lambda i, j: (0, 0, 0)),
                  pl.BlockSpec((None, D_MODEL, PEER_SLAB), lambda i, j: (j, 0, 0)),
                  pl.BlockSpec((None, D_MODEL, PEER_SLAB), lambda i, j: (j + PEER_STEPS, 0, 0)),
                  pl.BlockSpec((PEER_SLAB, D_MODEL), lambda i, j: (j, 0)),
                  pl.BlockSpec((PEER_SLAB, D_MODEL), lambda i, j: (j + PEER_STEPS, 0)),
                  vec, vec],
        out_specs=pl.BlockSpec((tm, D_MODEL), cur),
        out_shape=jax.ShapeDtypeStruct((n, D_MODEL), F32),
        scratch_shapes=[pltpu.VMEM((tm, D_MODEL), BF16),
                        pltpu.VMEM((tm, D_MODEL), BF16),
                        pltpu.VMEM((tm * GATE_PITCH, PEER_NKEYS), jnp.uint32),
                        pltpu.VMEM((tm, D_MODEL), F32),
                        pltpu.VMEM((2, 2, PEER_NKEYS, tps), F32),
                        pltpu.VMEM((tps // V7X_LANES, 3, PEER_TOPK, V7X_LANES), F32),
                        pltpu.VMEM((tps // V7X_LANES, 3, PEER_TOPK, V7X_LANES), F32),
                        sel, sel, sel, tok, tok, tok],
        compiler_params=pltpu.CompilerParams(
            dimension_semantics=("arbitrary", "arbitrary"), vmem_limit_bytes=_vmem_limit(est)),
        name="peer_route_mix_ln",
    )(x2d, x2d, wq, wq, keys, u_t, u_t, v_tab, v_tab, ln_g, ln_b)


def _rope_tables(seq):
    inv_freq = ROPE_THETA ** (-jnp.arange(0, ATTN_QKDIM, 2, dtype=F32) / ATTN_QKDIM)
    ang = jnp.arange(seq, dtype=F32)[:, None] * inv_freq[None, :]
    ang = jnp.concatenate([ang, ang], axis=-1)
    sign = jnp.where(jnp.arange(ATTN_QKDIM) < ROPE_HALF, -1.0, 1.0).astype(F32)
    reps = ATTN_WIDTH // ATTN_QKDIM
    return jnp.tile(jnp.cos(ang), (1, reps)), jnp.tile(jnp.sin(ang) * sign, (1, reps))


def _trunk(x, layers, depth):
    b, s, _ = x.shape
    alpha = (2 * depth) ** 0.25
    cos, sin = _rope_tables(s)
    x2d = x.reshape(b * s, D_MODEL)
    for l, p in enumerate(layers):
        lam_init = 0.8 - 0.6 * math.exp(-0.3 * l)
        yc, q, k, vt = _inproj(x2d, p["w_in"], p["wv_t"], p["conv_w"], cos, sin, s)
        shp = (b, s, ATTN_WIDTH)
        ya = _attention(q.reshape(shp), k.reshape(shp), vt,
                        p["lq1"], p["lk1"], p["lq2"], p["lk2"], p["subln_g"], lam_init)
        x2d = _outproj(yc, ya.reshape(b * s, ATTN_WIDTH), x2d, p["w_out"], p["ln1_g"], p["ln1_b"], alpha)
        x2d = _peer(x2d, p["wq"], p["keys"], p["u_t"], p["v"], p["ln2_g"], p["ln2_b"], alpha)
    return x2d.reshape(b, s, D_MODEL)


def kernel(x_prompt, x_sample, w_in, w_out, conv_w, lam_q1, lam_k1, lam_q2, lam_k2, subln_g, ln1_g, ln1_b,
           peer_wq, peer_keys, peer_u, peer_v, ln2_g, ln2_b):
    depth = w_in.shape[0]
    v_cols = slice(IN_COLS - ATTN_WIDTH, IN_COLS)

    def pair_order(tab):
        d = tab.shape[1]
        return tab.reshape(PEER_KEY_HALF, 2, PEER_NKEYS, d).transpose(1, 0, 2, 3).reshape(PEER_N, d)

    layers = []
    for l in range(depth):
        row = lambda a: a[l][None, :]
        layers.append(dict(
            w_in=w_in[l].astype(BF16), wv_t=w_in[l][:, v_cols].T.astype(BF16),
            w_out=w_out[l].astype(BF16), conv_w=conv_w[l],
            lq1=row(lam_q1), lk1=row(lam_k1), lq2=row(lam_q2), lk2=row(lam_k2), subln_g=row(subln_g),
            ln1_g=row(ln1_g), ln1_b=row(ln1_b), ln2_g=row(ln2_g), ln2_b=row(ln2_b),
            wq=peer_wq[l].astype(BF16), keys=peer_keys[l].astype(BF16),
            u_t=pair_order(peer_u[l].astype(BF16)).reshape(PEER_N // PEER_SLAB, PEER_SLAB, D_MODEL).transpose(0, 2, 1),
            v=pair_order(peer_v[l].astype(BF16))))
    return (_trunk(x_prompt, layers, depth), _trunk(x_sample, layers, depth))
```

```python
import functools
import math

import jax
import jax.numpy as jnp
from jax import lax
from jax.experimental import pallas as pl
from jax.experimental.pallas import tpu as pltpu

F32 = jnp.float32
BF16 = jnp.bfloat16

D_MODEL = 1024
CONV_WIDTH = 512
ATTN_WIDTH = 512
N_ATTN_HEADS = 4
ATTN_VDIM = 128
ATTN_QKDIM = 64
ROPE_HALF = ATTN_QKDIM // 2
IN_COLS = 3 * CONV_WIDTH + 3 * ATTN_WIDTH
ROPE_THETA = 10000.0
PEER_HEADS = 8
PEER_NKEYS = 128
PEER_N = PEER_NKEYS * PEER_NKEYS
PEER_HALF = 128
PEER_QCOLS = PEER_HEADS * 2 * PEER_HALF
PEER_TOPK = 16
PEER_SEL = PEER_HEADS * PEER_TOPK
LN_EPS = 1e-5

V7X_LANES = 128
V7X_SUBLANES = 8
V7X_VMEM_BYTES = 64 * 1024 * 1024
MIB = 1024 * 1024

INPROJ_ROWS = 512
ATTN_Q_ROWS = 256
ATTN_KEY_CHUNK = 512
ATTN_LOOKAHEAD = 6
OUTPROJ_ROWS = 512
PEER_ROWS = 512
PEER_SLAB = 512
PEER_GROUPS = PEER_SLAB // PEER_NKEYS
PEER_KEY_HALF = PEER_NKEYS // 2
PEER_STEPS = PEER_KEY_HALF // PEER_GROUPS
GATE_PITCH = PEER_KEY_HALF + V7X_SUBLANES
GATE_UNROLL = 64

NT_DIMS = (((1,), (1,)), ((), ()))


def _vmem_limit(nbytes):
    return int(min(V7X_VMEM_BYTES - 6 * MIB, max(32 * MIB, nbytes)))


def _layer_norm(z, g, b):
    mu = jnp.mean(z, axis=-1, keepdims=True)
    zc = z - mu
    var = jnp.mean(zc * zc, axis=-1, keepdims=True)
    return zc * lax.rsqrt(var + LN_EPS) * g + b


def _inproj_kernel(x_ref, xp_ref, xn_ref, w_ref, wvt_ref, cw_ref, cos_ref, sin_ref,
                   yc_ref, q_ref, k_ref, vt_ref, *, blocks_per_seq):
    tm = x_ref.shape[0]
    pos = lax.rem(pl.program_id(0), blocks_per_seq)
    xe = jnp.concatenate([x_ref[...], xp_ref[...], xn_ref[...]], axis=0).astype(BF16)
    xb = xe[:tm]

    def proj(v, c0, c1):
        return jnp.dot(v, w_ref[:, c0:c1], preferred_element_type=F32)

    g_b = proj(xb, 0, CONV_WIDTH)
    ue = proj(xe, CONV_WIDTH, 2 * CONV_WIDTH) * proj(xe, 2 * CONV_WIDTH, 3 * CONV_WIDTH)
    u = ue[:tm]
    u_prev = jnp.where(pos == 0, 0.0, ue[tm + V7X_SUBLANES - 1:tm + V7X_SUBLANES])
    u_next = jnp.where(pos == blocks_per_seq - 1, 0.0, ue[tm + V7X_SUBLANES:tm + V7X_SUBLANES + 1])
    row = lax.broadcasted_iota(jnp.int32, u.shape, 0)
    u_m1 = jnp.where(row == 0, u_prev, pltpu.roll(u, 1, 0))
    u_p1 = jnp.where(row == tm - 1, u_next, pltpu.roll(u, tm - 1, 0))
    conv = u_m1 * cw_ref[0:1, :] + u * cw_ref[1:2, :] + u_p1 * cw_ref[2:3, :]
    yc_ref[...] = (g_b * conv).astype(yc_ref.dtype)

    cos = cos_ref[...]
    sin = sin_ref[...]
    lane = lax.broadcasted_iota(jnp.int32, cos.shape, 1)
    first_half = jnp.bitwise_and(lane, ATTN_QKDIM - 1) < ROPE_HALF

    def rope(t):
        n = t.shape[1]
        rot = jnp.where(first_half, pltpu.roll(t, n - ROPE_HALF, 1), pltpu.roll(t, ROPE_HALF, 1))
        return t * cos + rot * sin

    c = 3 * CONV_WIDTH
    scale = math.log2(math.e) / math.sqrt(ATTN_QKDIM)
    q_ref[...] = (rope(proj(xb, c, c + ATTN_WIDTH)) * scale).astype(q_ref.dtype)
    k_ref[...] = rope(proj(xb, c + ATTN_WIDTH, c + 2 * ATTN_WIDTH)).astype(k_ref.dtype)
    vt_ref[...] = lax.dot_general(wvt_ref[...], xb, NT_DIMS, preferred_element_type=F32).astype(vt_ref.dtype)


def _inproj(x2d, w_in, wv_t, conv_w, cos, sin, seq):
    n = x2d.shape[0]
    tm = min(INPROJ_ROWS, seq)
    bps = seq // tm
    nblk = n // tm
    r8 = tm // V7X_SUBLANES
    last8 = n // V7X_SUBLANES - 1
    row_spec = lambda w: pl.BlockSpec((tm, w), lambda i: (i, 0))
    tab_spec = pl.BlockSpec((tm, ATTN_WIDTH), lambda i: (lax.rem(i, bps), 0))
    out = jax.ShapeDtypeStruct((n, CONV_WIDTH), BF16)
    est = 2 * (tm * D_MODEL * 4 + D_MODEL * IN_COLS * 2 + 2 * tm * ATTN_WIDTH * 4 + 4 * tm * 512 * 2) \
        + 12 * tm * 512 * 4 + 2 * tm * D_MODEL * 4
    return pl.pallas_call(
        functools.partial(_inproj_kernel, blocks_per_seq=bps),
        grid=(nblk,),
        in_specs=[
            row_spec(D_MODEL),
            pl.BlockSpec((V7X_SUBLANES, D_MODEL), lambda i: (jnp.maximum(i * r8 - 1, 0), 0)),
            pl.BlockSpec((V7X_SUBLANES, D_MODEL), lambda i: (jnp.minimum((i + 1) * r8, last8), 0)),
            pl.BlockSpec((D_MODEL, IN_COLS), lambda i: (0, 0)),
            pl.BlockSpec((ATTN_WIDTH, D_MODEL), lambda i: (0, 0)),
            pl.BlockSpec((3, CONV_WIDTH), lambda i: (0, 0)),
            tab_spec, tab_spec,
        ],
        out_specs=[row_spec(CONV_WIDTH)] * 3 + [pl.BlockSpec((ATTN_WIDTH, tm), lambda i: (0, i))],
        out_shape=[out] * 3 + [jax.ShapeDtypeStruct((ATTN_WIDTH, n), BF16)],
        compiler_params=pltpu.CompilerParams(
            dimension_semantics=("parallel",), vmem_limit_bytes=_vmem_limit(est)),
        name="inproj_conv_rope",
    )(x2d, x2d, x2d, w_in, wv_t, conv_w, cos, sin)


def _attn_kernel(q_ref, k_ref, vt_ref, lq1_ref, lk1_ref, lq2_ref, lk2_ref, g_ref, o_ref, *, lam_init):
    q = q_ref[...]
    s = k_ref.shape[0]
    ck = min(ATTN_KEY_CHUNK, s)
    lane = lax.broadcasted_iota(jnp.int32, q.shape, 1)
    zero = jnp.zeros_like(q)
    q_maps = (jnp.where(lane < ATTN_QKDIM, q, zero), jnp.where(lane >= ATTN_QKDIM, q, zero))

    m_run, l_run, o_run = [None, None], [None, None], [None, None]
    items = [(c, mp) for c in range(s // ck) for mp in range(2)]

    def scores(item):
        c, mp = item
        return lax.dot_general(k_ref[c * ck:(c + 1) * ck, :], q_maps[mp], NT_DIMS,
                               preferred_element_type=F32)

    pending = [scores(it) for it in items[:ATTN_LOOKAHEAD]]
    for n, (c, mp) in enumerate(items):
        keys = slice(c * ck, (c + 1) * ck)
        if n + ATTN_LOOKAHEAD < len(items):
            pending.append(scores(items[n + ATTN_LOOKAHEAD]))
        st = pending.pop(0)
        m_c = jnp.max(st, axis=0, keepdims=True)
        m_new = m_c if c == 0 else jnp.maximum(m_run[mp], m_c)
        e = jnp.exp2(st - m_new)
        l_c = jnp.sum(e, axis=0, keepdims=True)
        pv = jnp.dot(vt_ref[:, keys], e.astype(vt_ref.dtype), preferred_element_type=F32)
        if c == 0:
            l_run[mp], o_run[mp] = l_c, pv
        else:
            rescale = jnp.exp2(m_run[mp] - m_new)
            l_run[mp] = l_run[mp] * rescale + l_c
            o_run[mp] = o_run[mp] * rescale + pv
        m_run[mp] = m_new

    lam = (jnp.exp(jnp.sum(lq1_ref[...] * lk1_ref[...], axis=-1, keepdims=True))
           - jnp.exp(jnp.sum(lq2_ref[...] * lk2_ref[...], axis=-1, keepdims=True)) + lam_init)
    ot = o_run[0] / l_run[0] - lam * (o_run[1] / l_run[1])
    ot = ot * lax.rsqrt(jnp.mean(ot * ot, axis=0, keepdims=True) + LN_EPS)
    o_ref[...] = (ot.T * g_ref[...] * (1.0 - lam_init)).astype(o_ref.dtype)


def _attention(q, k, vt, lq1, lk1, lq2, lk2, subln_g, lam_init):
    b, s, _ = q.shape
    tq = min(ATTN_Q_ROWS, s)
    q_spec = pl.BlockSpec((None, tq, ATTN_VDIM), lambda bi, h, i: (bi, i, h))
    k_spec = pl.BlockSpec((None, s, ATTN_VDIM), lambda bi, h, i: (bi, 0, h))
    vt_spec = pl.BlockSpec((ATTN_VDIM, s), lambda bi, h, i: (h, bi))
    vec = lambda w: pl.BlockSpec((1, w), lambda bi, h, i: (0, 0))
    est = 4 * s * ATTN_VDIM * 2 + 2 * tq * s * (4 + 4 + 2) + 8 * tq * ATTN_VDIM * 4
    return pl.pallas_call(
        functools.partial(_attn_kernel, lam_init=lam_init),
        grid=(b, N_ATTN_HEADS, s // tq),
        in_specs=[q_spec, k_spec, vt_spec, vec(ATTN_QKDIM), vec(ATTN_QKDIM), vec(ATTN_QKDIM),
                  vec(ATTN_QKDIM), vec(ATTN_VDIM)],
        out_specs=q_spec,
        out_shape=jax.ShapeDtypeStruct((b, s, ATTN_WIDTH), BF16),
        compiler_params=pltpu.CompilerParams(
            dimension_semantics=("parallel", "parallel", "parallel"), vmem_limit_bytes=_vmem_limit(est)),
        name="diff_attention",
    )(q, k, vt, lq1, lk1, lq2, lk2, subln_g)


def _outproj_kernel(yc_ref, ya_ref, x_ref, w_ref, g_ref, b_ref, o_ref, *, alpha):
    m = jnp.dot(yc_ref[...], w_ref[:CONV_WIDTH, :], preferred_element_type=F32)
    m = m + jnp.dot(ya_ref[...], w_ref[CONV_WIDTH:, :], preferred_element_type=F32)
    o_ref[...] = _layer_norm(alpha * x_ref[...] + m, g_ref[...], b_ref[...])


def _outproj(yc, ya, x2d, w_out, g, b, alpha):
    n = x2d.shape[0]
    tm = min(OUTPROJ_ROWS, n)
    row_spec = lambda w: pl.BlockSpec((tm, w), lambda i: (i, 0))
    vec = pl.BlockSpec((1, D_MODEL), lambda i: (0, 0))
    est = 2 * (2 * tm * 512 * 2 + 2 * tm * D_MODEL * 4 + D_MODEL * D_MODEL * 2) + 4 * tm * D_MODEL * 4
    return pl.pallas_call(
        functools.partial(_outproj_kernel, alpha=alpha),
        grid=(n // tm,),
        in_specs=[row_spec(CONV_WIDTH), row_spec(ATTN_WIDTH), row_spec(D_MODEL),
                  pl.BlockSpec((D_MODEL, D_MODEL), lambda i: (0, 0)), vec, vec],
        out_specs=row_spec(D_MODEL),
        out_shape=jax.ShapeDtypeStruct((n, D_MODEL), F32),
        compiler_params=pltpu.CompilerParams(
            dimension_semantics=("parallel",), vmem_limit_bytes=_vmem_limit(est)),
        name="outproj_ln",
    )(yc, ya, x2d, w_out, g, b)


def _row_index(n, lanes):
    return lax.broadcasted_iota(jnp.int32, (n, lanes), 0).astype(F32)


def _top16(s, keys, store_val, store_key):
    for r in range(PEER_TOPK):
        m = jnp.max(s, axis=0, keepdims=True)
        key = jnp.min(jnp.where(s == m, keys, jnp.inf), axis=0, keepdims=True)
        s = jnp.where(keys == key, -jnp.inf, s)
        store_val(r, m)
        store_key(r, key)


def _top16_by_row(s, store_val, store_key):
    n, lanes = s.shape
    nslab = n // V7X_SUBLANES
    sublane = _row_index(V7X_SUBLANES, lanes)
    slab_keys = [sublane * float(nslab) + float(a) for a in range(nslab)]
    keys = jnp.concatenate(slab_keys, axis=0)
    for r in range(PEER_TOPK):
        vals = [s[a * V7X_SUBLANES:(a + 1) * V7X_SUBLANES, :] for a in range(nslab)]
        idxs = slab_keys
        while len(vals) > 1:
            nxt_v, nxt_i = [], []
            for a in range(0, len(vals), 2):
                left = vals[a] >= vals[a + 1]
                nxt_v.append(jnp.maximum(vals[a], vals[a + 1]))
                nxt_i.append(jnp.where(left, idxs[a], idxs[a + 1]))
            vals, idxs = nxt_v, nxt_i
        v8, k8 = vals[0], idxs[0]
        for step in (1, 2, 4):
            pv = pltpu.roll(v8, V7X_SUBLANES - step, 0)
            pk = pltpu.roll(k8, V7X_SUBLANES - step, 0)
            left = v8 >= pv
            v8 = jnp.maximum(v8, pv)
            k8 = jnp.where(left, k8, pk)
        m = v8[0:1, :]
        key = k8[0:1, :]
        s = jnp.where(keys == key, -jnp.inf, s)
        store_val(r, m)
        store_key(r, key)
    return m


def _zero_after(x):
    bits = lax.bitcast_convert_type(x, jnp.uint32)
    return jnp.right_shift(jnp.right_shift(bits, jnp.uint32(16)), jnp.uint32(16)).astype(F32)


def _pair_candidates(s1, s2):
    lanes = s1.shape[1]
    row16 = _row_index(PEER_TOPK, lanes)
    row8 = row16[:V7X_SUBLANES]
    sums, keys = [], []
    for a in range(2):
        sums.append(s1[a:a + 1, :] + s2)
        keys.append(row16 + float(a * PEER_TOPK))
    s1_tail = jnp.where(row16 >= 2.0, s1, -jnp.inf)
    for b in range(2):
        sums.append(s1_tail + s2[b:b + 1, :])
        keys.append(row16 * float(PEER_TOPK) + float(b))
    for b in range(2, 5):
        sums.append(s1_tail[:V7X_SUBLANES] + s2[b:b + 1, :])
        keys.append(row8 * float(PEER_TOPK) + float(b))
    return jnp.concatenate(sums, axis=0), jnp.concatenate(keys, axis=0)


def _route_chunk(sc_ref, lanes, tv_ref, tp_ref, slot, after):
    def store(ref, which):
        def f(r, val):
            ref[slot, which, r:r + 1, :] = val
        return f

    last = _top16_by_row(sc_ref[0, :, lanes], store(tv_ref, 0), store(tp_ref, 0))
    _top16_by_row(sc_ref[1, :, lanes] + _zero_after(last), store(tv_ref, 1), store(tp_ref, 1))
    comb, comb_keys = _pair_candidates(tv_ref[slot, 0], tv_ref[slot, 1])
    if after is not None:
        comb = comb + after
    _top16(comb, comb_keys, store(tv_ref, 2), store(tp_ref, 2))
    top_s = tv_ref[slot, 2]
    pos = tp_ref[slot, 2].astype(jnp.int32)
    a_sel = jnp.right_shift(pos, 4)
    b_sel = jnp.bitwise_and(pos, PEER_TOPK - 1)
    i1 = tp_ref[slot, 0]
    i2 = tp_ref[slot, 1]
    e1 = jnp.zeros_like(top_s)
    e2 = jnp.zeros_like(top_s)
    for a in range(PEER_TOPK):
        e1 = jnp.where(a_sel == a, i1[a:a + 1, :], e1)
        e2 = jnp.where(b_sel == a, i2[a:a + 1, :], e2)
    ex = jnp.exp(top_s - top_s[0:1, :])
    return e1, e2, ex / jnp.sum(ex, axis=0, keepdims=True)


def _subkey_scores(xb_ref, part, tps, wq_ref, keys_ref, out_ref):
    x_part = xb_ref[pl.ds(pl.multiple_of(part * tps, tps), tps), :]
    q = jnp.dot(x_part, wq_ref[...], preferred_element_type=F32)
    for p in range(2):
        out_ref[p] = lax.dot_general(keys_ref[p], q[:, p * PEER_HALF:(p + 1) * PEER_HALF].astype(BF16), NT_DIMS,
                                     preferred_element_type=F32)


def _peer_kernel(x_ref, xn_ref, wq_ref, keys_ref, ut_lo_ref, ut_hi_ref, v_lo_ref, v_hi_ref,
                 lng_ref, lnb_ref, o_ref,
                 xb_ref, xnb_ref, gate_ref, acc_ref, sc_ref, tv_ref, tp_ref, sel1_ref, sel2_ref, selg_ref,
                 tok1_ref, tok2_ref, tokg_ref, *, alpha):
    tm = x_ref.shape[0]
    i = pl.program_id(0)
    j = pl.program_id(1)
    parts = sel1_ref.shape[0]
    tps = sel1_ref.shape[2]

    @pl.when(j == 0)
    def _():
        xb_ref[...] = x_ref[...].astype(BF16)
        xnb_ref[...] = xn_ref[...].astype(BF16)
        acc_ref[...] = jnp.zeros_like(acc_ref)
        _subkey_scores(xnb_ref, 0, tps, wq_ref.at[0], keys_ref, sc_ref.at[0])

        @pl.when(i == 0)
        def _():
            gate_ref[...] = jnp.zeros_like(gate_ref)

        @pl.when(i > 0)
        def _():
            for part in range(parts):
                for c in range(tps // V7X_LANES):
                    lanes = slice(c * V7X_LANES, (c + 1) * V7X_LANES)
                    rows = slice(part * tps + c * V7X_LANES, part * tps + (c + 1) * V7X_LANES)
                    tok1_ref[rows, :] = sel1_ref[part, :, lanes].T
                    tok2_ref[rows, :] = sel2_ref[part, :, lanes].T
                    tokg_ref[rows, :] = selg_ref[part, :, lanes].T
            key_row = lax.broadcasted_iota(jnp.int32, (PEER_NKEYS, PEER_SEL), 0).astype(F32).astype(BF16)
            one = jnp.ones((PEER_NKEYS, PEER_SEL), BF16)
            zero = jnp.zeros((PEER_NKEYS, PEER_SEL), BF16)

            def token_body(t, carry):
                r1 = tok1_ref[pl.ds(t, 1), :].astype(BF16)
                r2 = tok2_ref[pl.ds(t, 1), :].astype(BF16)
                gg = jnp.broadcast_to(tokg_ref[pl.ds(t, 1), :].astype(BF16), (PEER_NKEYS, PEER_SEL))
                hot1 = jnp.where(key_row == r1, one, zero)
                wgt2 = jnp.where(key_row == r2, gg, zero)
                tile = lax.dot_general(hot1, wgt2, NT_DIMS, preferred_element_type=F32)
                packed = pltpu.bitcast(tile.astype(BF16), jnp.uint32)
                gate_ref[pl.ds(pl.multiple_of(t * GATE_PITCH, V7X_SUBLANES), PEER_KEY_HALF), :] = packed
                return carry

            lax.fori_loop(0, tm, token_body, 0, unroll=GATE_UNROLL)

    head = j // parts
    part = j % parts
    sc_cur = sc_ref.at[j % 2]
    base = pl.multiple_of(head * PEER_TOPK, PEER_TOPK)
    after = None
    for c in range(tps // V7X_LANES):
        lanes = slice(c * V7X_LANES, (c + 1) * V7X_LANES)
        e1, e2, gate = _route_chunk(sc_cur, lanes, tv_ref, tp_ref, c, after)
        sel1_ref[part, pl.ds(base, PEER_TOPK), lanes] = e1
        sel2_ref[part, pl.ds(base, PEER_TOPK), lanes] = e2
        selg_ref[part, pl.ds(base, PEER_TOPK), lanes] = gate
        after = _zero_after(e1[0:1, :] + e2[0:1, :] + gate[0:1, :])

    xb = xb_ref[...]
    h_lo = jnp.dot(xb, ut_lo_ref[...], preferred_element_type=F32)
    h_hi = jnp.dot(xb, ut_hi_ref[...], preferred_element_type=F32)
    parts_lo, parts_hi = [], []
    for r in range(PEER_GROUPS):
        word = gate_ref[pl.ds(j * PEER_GROUPS + r, tm, stride=GATE_PITCH), :]
        cols = slice(r * PEER_NKEYS, (r + 1) * PEER_NKEYS)
        gate_lo = lax.bitcast_convert_type(jnp.left_shift(word, jnp.uint32(16)), F32)
        gate_hi = lax.bitcast_convert_type(jnp.bitwise_and(word, jnp.uint32(0xFFFF0000)), F32)
        for h, gate, plist in ((h_lo, gate_lo, parts_lo), (h_hi, gate_hi, parts_hi)):
            hr = h[:, cols]
            gelu2 = hr * (1.0 + lax.erf(hr * math.sqrt(0.5)))
            plist.append((gelu2 * gate).astype(BF16))
    acc = jnp.dot(jnp.concatenate(parts_lo, axis=1), v_lo_ref[...], preferred_element_type=F32)
    acc = acc + jnp.dot(jnp.concatenate(parts_hi, axis=1), v_hi_ref[...], preferred_element_type=F32)
    acc_ref[...] += acc

    nxt = jnp.minimum(j + 1, pl.num_programs(1) - 1)
    _subkey_scores(xnb_ref, nxt % parts, tps, wq_ref.at[nxt // parts], keys_ref, sc_ref.at[(j + 1) % 2])

    @pl.when(j == pl.num_programs(1) - 1)
    def _():
        o_ref[...] = _layer_norm(alpha * x_ref[...] + acc_ref[...], lng_ref[...], lnb_ref[...])


def _peer(x2d, wq, keys, u_t, v_tab, ln_g, ln_b, alpha):
    n = x2d.shape[0]
    tm = min(PEER_ROWS, n)
    nb = n // tm
    assert PEER_STEPS % PEER_HEADS == 0
    parts = PEER_STEPS // PEER_HEADS
    tps = tm // parts
    assert tps % V7X_LANES == 0
    once = pl.Buffered(1)
    vec = pl.BlockSpec((1, D_MODEL), lambda i, j: (0, 0))
    cur = lambda i, j: (jnp.maximum(i - 1, 0), 0)
    gate_bytes = tm * GATE_PITCH * PEER_NKEYS * 4
    est = gate_bytes + 3 * tm * D_MODEL * 4 + 8 * PEER_SLAB * D_MODEL * 2 + tm * D_MODEL * 8 \
        + 8 * tm * PEER_SLAB * 4 + 16 * tm * PEER_SEL * 4
    sel = pltpu.VMEM((parts, PEER_SEL, tps), F32)
    tok = pltpu.VMEM((tm, PEER_SEL), F32)
    return pl.pallas_call(
        functools.partial(_peer_kernel, alpha=alpha),
        grid=(nb + 1, PEER_STEPS),
        in_specs=[pl.BlockSpec((tm, D_MODEL), cur, pipeline_mode=once),
                  pl.BlockSpec((tm, D_MODEL), lambda i, j: (jnp.minimum(i, nb - 1), 0), pipeline_mode=once),
                  pl.BlockSpec((PEER_HEADS, D_MODEL, 2 * PEER_HALF), lambda i, j: (0, 0, 0), pipeline_mode=once),
                  pl.BlockSpec((2, PEER_NKEYS, PEER_HALF), lambda i, j: (0, 0, 0)),
                  pl.BlockSpec((None, D_MODEL, PEER_SLAB), lambda i, j: (j, 0, 0)),
                  pl.BlockSpec((None, D_MODEL, PEER_SLAB), lambda i, j: (j + PEER_STEPS, 0, 0)),
                  pl.BlockSpec((PEER_SLAB, D_MODEL), lambda i, j: (j, 0)),
                  pl.BlockSpec((PEER_SLAB, D_MODEL), lambda i, j: (j + PEER_STEPS, 0)),
                  vec, vec],
        out_specs=pl.BlockSpec((tm, D_MODEL), cur),
        out_shape=jax.ShapeDtypeStruct((n, D_MODEL), F32),
        scratch_shapes=[pltpu.VMEM((tm, D_MODEL), BF16),
                        pltpu.VMEM((tm, D_MODEL), BF16),
                        pltpu.VMEM((tm * GATE_PITCH, PEER_NKEYS), jnp.uint32),
                        pltpu.VMEM((tm, D_MODEL), F32),
                        pltpu.VMEM((2, 2, PEER_NKEYS, tps), F32),
                        pltpu.VMEM((tps // V7X_LANES, 3, PEER_TOPK, V7X_LANES), F32),
                        pltpu.VMEM((tps // V7X_LANES, 3, PEER_TOPK, V7X_LANES), F32),
                        sel, sel, sel, tok, tok, tok],
        compiler_params=pltpu.CompilerParams(
            dimension_semantics=("arbitrary", "arbitrary"), vmem_limit_bytes=_vmem_limit(est)),
        name="peer_route_mix_ln",
    )(x2d, x2d, wq, keys, u_t, u_t, v_tab, v_tab, ln_g, ln_b)


def _rope_tables(seq):
    inv_freq = ROPE_THETA ** (-jnp.arange(0, ATTN_QKDIM, 2, dtype=F32) / ATTN_QKDIM)
    ang = jnp.arange(seq, dtype=F32)[:, None] * inv_freq[None, :]
    ang = jnp.concatenate([ang, ang], axis=-1)
    sign = jnp.where(jnp.arange(ATTN_QKDIM) < ROPE_HALF, -1.0, 1.0).astype(F32)
    reps = ATTN_WIDTH // ATTN_QKDIM
    return jnp.tile(jnp.cos(ang), (1, reps)), jnp.tile(jnp.sin(ang) * sign, (1, reps))


def _trunk(x, layers, depth):
    b, s, _ = x.shape
    alpha = (2 * depth) ** 0.25
    cos, sin = _rope_tables(s)
    x2d = x.reshape(b * s, D_MODEL)
    for l, p in enumerate(layers):
        lam_init = 0.8 - 0.6 * math.exp(-0.3 * l)
        yc, q, k, vt = _inproj(x2d, p["w_in"], p["wv_t"], p["conv_w"], cos, sin, s)
        shp = (b, s, ATTN_WIDTH)
        ya = _attention(q.reshape(shp), k.reshape(shp), vt,
                        p["lq1"], p["lk1"], p["lq2"], p["lk2"], p["subln_g"], lam_init)
        x2d = _outproj(yc, ya.reshape(b * s, ATTN_WIDTH), x2d, p["w_out"], p["ln1_g"], p["ln1_b"], alpha)
        x2d = _peer(x2d, p["wq"], p["keys"], p["u_t"], p["v"], p["ln2_g"], p["ln2_b"], alpha)
    return x2d.reshape(b, s, D_MODEL)


def kernel(x_prompt, x_sample, w_in, w_out, conv_w, lam_q1, lam_k1, lam_q2, lam_k2, subln_g, ln1_g, ln1_b,
           peer_wq, peer_keys, peer_u, peer_v, ln2_g, ln2_b):
    depth = w_in.shape[0]
    v_cols = slice(IN_COLS - ATTN_WIDTH, IN_COLS)

    def pair_order(tab):
        d = tab.shape[1]
        return tab.reshape(PEER_KEY_HALF, 2, PEER_NKEYS, d).transpose(1, 0, 2, 3).reshape(PEER_N, d)

    layers = []
    for l in range(depth):
        row = lambda a: a[l][None, :]
        layers.append(dict(
            w_in=w_in[l].astype(BF16), wv_t=w_in[l][:, v_cols].T.astype(BF16),
            w_out=w_out[l].astype(BF16), conv_w=conv_w[l],
            lq1=row(lam_q1), lk1=row(lam_k1), lq2=row(lam_q2), lk2=row(lam_k2), subln_g=row(subln_g),
            ln1_g=row(ln1_g), ln1_b=row(ln1_b), ln2_g=row(ln2_g), ln2_b=row(ln2_b),
            wq=peer_wq[l].astype(BF16).reshape(D_MODEL, PEER_HEADS, 2 * PEER_HALF).transpose(1, 0, 2),
            keys=peer_keys[l].astype(BF16).reshape(2, V7X_SUBLANES, PEER_NKEYS // V7X_SUBLANES, PEER_HALF)
            .transpose(0, 2, 1, 3).reshape(2, PEER_NKEYS, PEER_HALF),
            u_t=pair_order(peer_u[l].astype(BF16)).reshape(PEER_N // PEER_SLAB, PEER_SLAB, D_MODEL).transpose(0, 2, 1),
            v=pair_order((0.5 * peer_v[l]).astype(BF16))))
    return (_trunk(x_prompt, layers, depth), _trunk(x_sample, layers, depth))
```

```python
import functools
import math

import jax
import jax.numpy as jnp
from jax import lax
from jax.experimental import pallas as pl
from jax.experimental.pallas import tpu as pltpu

F32 = jnp.float32
BF16 = jnp.bfloat16

D_MODEL = 1024
CONV_WIDTH = 512
ATTN_WIDTH = 512
N_ATTN_HEADS = 4
ATTN_VDIM = 128
ATTN_QKDIM = 64
ROPE_HALF = ATTN_QKDIM // 2
IN_COLS = 3 * CONV_WIDTH + 3 * ATTN_WIDTH
ROPE_THETA = 10000.0
PEER_HEADS = 8
PEER_NKEYS = 128
PEER_N = PEER_NKEYS * PEER_NKEYS
PEER_HALF = 128
PEER_QCOLS = PEER_HEADS * 2 * PEER_HALF
PEER_TOPK = 16
PEER_SEL = PEER_HEADS * PEER_TOPK
LN_EPS = 1e-5

V7X_LANES = 128
V7X_SUBLANES = 8
V7X_VMEM_BYTES = 64 * 1024 * 1024
MIB = 1024 * 1024

INPROJ_ROWS = 512
ATTN_Q_ROWS = 256
ATTN_KEY_CHUNK = 512
ATTN_LOOKAHEAD = 6
OUTPROJ_ROWS = 512
PEER_ROWS = 512
PEER_SLAB = 512
PEER_GROUPS = PEER_SLAB // PEER_NKEYS
PEER_KEY_HALF = PEER_NKEYS // 2
PEER_STEPS = PEER_KEY_HALF // PEER_GROUPS
GATE_PITCH = PEER_KEY_HALF + V7X_SUBLANES
GATE_UNROLL = 128

NT_DIMS = (((1,), (1,)), ((), ()))


def _vmem_limit(nbytes):
    return int(min(V7X_VMEM_BYTES - 6 * MIB, max(32 * MIB, nbytes)))


def _layer_norm(z, g, b):
    mu = jnp.mean(z, axis=-1, keepdims=True)
    zc = z - mu
    var = jnp.mean(zc * zc, axis=-1, keepdims=True)
    return zc * lax.rsqrt(var + LN_EPS) * g + b


def _inproj_kernel(x_ref, xp_ref, xn_ref, w_ref, wvt_ref, cw_ref, cos_ref, sin_ref,
                   yc_ref, q_ref, k_ref, vt_ref, *, blocks_per_seq):
    tm = x_ref.shape[0]
    pos = lax.rem(pl.program_id(0), blocks_per_seq)
    xe = jnp.concatenate([x_ref[...], xp_ref[...], xn_ref[...]], axis=0).astype(BF16)
    xb = xe[:tm]

    def proj(v, c0, c1):
        return jnp.dot(v, w_ref[:, c0:c1], preferred_element_type=F32)

    g_b = proj(xb, 0, CONV_WIDTH)
    ue = proj(xe, CONV_WIDTH, 2 * CONV_WIDTH) * proj(xe, 2 * CONV_WIDTH, 3 * CONV_WIDTH)
    u = ue[:tm]
    u_prev = jnp.where(pos == 0, 0.0, ue[tm + V7X_SUBLANES - 1:tm + V7X_SUBLANES])
    u_next = jnp.where(pos == blocks_per_seq - 1, 0.0, ue[tm + V7X_SUBLANES:tm + V7X_SUBLANES + 1])
    row = lax.broadcasted_iota(jnp.int32, u.shape, 0)
    u_m1 = jnp.where(row == 0, u_prev, pltpu.roll(u, 1, 0))
    u_p1 = jnp.where(row == tm - 1, u_next, pltpu.roll(u, tm - 1, 0))
    conv = u_m1 * cw_ref[0:1, :] + u * cw_ref[1:2, :] + u_p1 * cw_ref[2:3, :]
    yc_ref[...] = (g_b * conv).astype(yc_ref.dtype)

    cos = cos_ref[...]
    sin = sin_ref[...]
    lane = lax.broadcasted_iota(jnp.int32, cos.shape, 1)
    first_half = jnp.bitwise_and(lane, ATTN_QKDIM - 1) < ROPE_HALF

    def rope(t):
        n = t.shape[1]
        rot = jnp.where(first_half, pltpu.roll(t, n - ROPE_HALF, 1), pltpu.roll(t, ROPE_HALF, 1))
        return t * cos + rot * sin

    c = 3 * CONV_WIDTH
    scale = math.log2(math.e) / math.sqrt(ATTN_QKDIM)
    q_ref[...] = (rope(proj(xb, c, c + ATTN_WIDTH)) * scale).astype(q_ref.dtype)
    k_ref[...] = rope(proj(xb, c + ATTN_WIDTH, c + 2 * ATTN_WIDTH)).astype(k_ref.dtype)
    vt_ref[...] = lax.dot_general(wvt_ref[...], xb, NT_DIMS, preferred_element_type=F32).astype(vt_ref.dtype)


def _inproj(x2d, w_in, wv_t, conv_w, cos, sin, seq):
    n = x2d.shape[0]
    tm = min(INPROJ_ROWS, seq)
    bps = seq // tm
    nblk = n // tm
    r8 = tm // V7X_SUBLANES
    last8 = n // V7X_SUBLANES - 1
    row_spec = lambda w: pl.BlockSpec((tm, w), lambda i: (i, 0))
    tab_spec = pl.BlockSpec((tm, ATTN_WIDTH), lambda i: (lax.rem(i, bps), 0))
    out = jax.ShapeDtypeStruct((n, CONV_WIDTH), BF16)
    est = 2 * (tm * D_MODEL * 4 + D_MODEL * IN_COLS * 2 + 2 * tm * ATTN_WIDTH * 4 + 4 * tm * 512 * 2) \
        + 12 * tm * 512 * 4 + 2 * tm * D_MODEL * 4
    return pl.pallas_call(
        functools.partial(_inproj_kernel, blocks_per_seq=bps),
        grid=(nblk,),
        in_specs=[
            row_spec(D_MODEL),
            pl.BlockSpec((V7X_SUBLANES, D_MODEL), lambda i: (jnp.maximum(i * r8 - 1, 0), 0)),
            pl.BlockSpec((V7X_SUBLANES, D_MODEL), lambda i: (jnp.minimum((i + 1) * r8, last8), 0)),
            pl.BlockSpec((D_MODEL, IN_COLS), lambda i: (0, 0)),
            pl.BlockSpec((ATTN_WIDTH, D_MODEL), lambda i: (0, 0)),
            pl.BlockSpec((3, CONV_WIDTH), lambda i: (0, 0)),
            tab_spec, tab_spec,
        ],
        out_specs=[row_spec(CONV_WIDTH)] * 3 + [pl.BlockSpec((ATTN_WIDTH, tm), lambda i: (0, i))],
        out_shape=[out] * 3 + [jax.ShapeDtypeStruct((ATTN_WIDTH, n), BF16)],
        compiler_params=pltpu.CompilerParams(
            dimension_semantics=("parallel",), vmem_limit_bytes=_vmem_limit(est)),
        name="inproj_conv_rope",
    )(x2d, x2d, x2d, w_in, wv_t, conv_w, cos, sin)


def _attn_kernel(q_ref, k_ref, vt_ref, lq1_ref, lk1_ref, lq2_ref, lk2_ref, g_ref, o_ref, *, lam_init):
    q = q_ref[...]
    s = k_ref.shape[0]
    ck = min(ATTN_KEY_CHUNK, s)
    lane = lax.broadcasted_iota(jnp.int32, q.shape, 1)
    zero = jnp.zeros_like(q)
    q_maps = (jnp.where(lane < ATTN_QKDIM, q, zero), jnp.where(lane >= ATTN_QKDIM, q, zero))

    m_run, l_run, o_run = [None, None], [None, None], [None, None]
    items = [(c, mp) for c in range(s // ck) for mp in range(2)]

    def scores(item):
        c, mp = item
        return lax.dot_general(k_ref[c * ck:(c + 1) * ck, :], q_maps[mp], NT_DIMS,
                               preferred_element_type=F32)

    pending = [scores(it) for it in items[:ATTN_LOOKAHEAD]]
    for n, (c, mp) in enumerate(items):
        keys = slice(c * ck, (c + 1) * ck)
        if n + ATTN_LOOKAHEAD < len(items):
            pending.append(scores(items[n + ATTN_LOOKAHEAD]))
        st = pending.pop(0)
        m_c = jnp.max(st, axis=0, keepdims=True)
        m_new = m_c if c == 0 else jnp.maximum(m_run[mp], m_c)
        e = jnp.exp2(st - m_new)
        l_c = jnp.sum(e, axis=0, keepdims=True)
        pv = jnp.dot(vt_ref[:, keys], e.astype(vt_ref.dtype), preferred_element_type=F32)
        if c == 0:
            l_run[mp], o_run[mp] = l_c, pv
        else:
            rescale = jnp.exp2(m_run[mp] - m_new)
            l_run[mp] = l_run[mp] * rescale + l_c
            o_run[mp] = o_run[mp] * rescale + pv
        m_run[mp] = m_new

    lam = (jnp.exp(jnp.sum(lq1_ref[...] * lk1_ref[...], axis=-1, keepdims=True))
           - jnp.exp(jnp.sum(lq2_ref[...] * lk2_ref[...], axis=-1, keepdims=True)) + lam_init)
    ot = o_run[0] / l_run[0] - lam * (o_run[1] / l_run[1])
    ot = ot * lax.rsqrt(jnp.mean(ot * ot, axis=0, keepdims=True) + LN_EPS)
    o_ref[...] = (ot.T * g_ref[...] * (1.0 - lam_init)).astype(o_ref.dtype)


def _attention(q, k, vt, lq1, lk1, lq2, lk2, subln_g, lam_init):
    b, s, _ = q.shape
    tq = min(ATTN_Q_ROWS, s)
    q_spec = pl.BlockSpec((None, tq, ATTN_VDIM), lambda bi, h, i: (bi, i, h))
    k_spec = pl.BlockSpec((None, s, ATTN_VDIM), lambda bi, h, i: (bi, 0, h))
    vt_spec = pl.BlockSpec((ATTN_VDIM, s), lambda bi, h, i: (h, bi))
    vec = lambda w: pl.BlockSpec((1, w), lambda bi, h, i: (0, 0))
    est = 4 * s * ATTN_VDIM * 2 + 2 * tq * s * (4 + 4 + 2) + 8 * tq * ATTN_VDIM * 4
    return pl.pallas_call(
        functools.partial(_attn_kernel, lam_init=lam_init),
        grid=(b, N_ATTN_HEADS, s // tq),
        in_specs=[q_spec, k_spec, vt_spec, vec(ATTN_QKDIM), vec(ATTN_QKDIM), vec(ATTN_QKDIM),
                  vec(ATTN_QKDIM), vec(ATTN_VDIM)],
        out_specs=q_spec,
        out_shape=jax.ShapeDtypeStruct((b, s, ATTN_WIDTH), BF16),
        compiler_params=pltpu.CompilerParams(
            dimension_semantics=("parallel", "parallel", "parallel"), vmem_limit_bytes=_vmem_limit(est)),
        name="diff_attention",
    )(q, k, vt, lq1, lk1, lq2, lk2, subln_g)


def _outproj_kernel(yc_ref, ya_ref, x_ref, w_ref, g_ref, b_ref, o_ref, *, alpha):
    m = jnp.dot(yc_ref[...], w_ref[:CONV_WIDTH, :], preferred_element_type=F32)
    m = m + jnp.dot(ya_ref[...], w_ref[CONV_WIDTH:, :], preferred_element_type=F32)
    o_ref[...] = _layer_norm(alpha * x_ref[...] + m, g_ref[...], b_ref[...])


def _outproj(yc, ya, x2d, w_out, g, b, alpha):
    n = x2d.shape[0]
    tm = min(OUTPROJ_ROWS, n)
    row_spec = lambda w: pl.BlockSpec((tm, w), lambda i: (i, 0))
    vec = pl.BlockSpec((1, D_MODEL), lambda i: (0, 0))
    est = 2 * (2 * tm * 512 * 2 + 2 * tm * D_MODEL * 4 + D_MODEL * D_MODEL * 2) + 4 * tm * D_MODEL * 4
    return pl.pallas_call(
        functools.partial(_outproj_kernel, alpha=alpha),
        grid=(n // tm,),
        in_specs=[row_spec(CONV_WIDTH), row_spec(ATTN_WIDTH), row_spec(D_MODEL),
                  pl.BlockSpec((D_MODEL, D_MODEL), lambda i: (0, 0)), vec, vec],
        out_specs=row_spec(D_MODEL),
        out_shape=jax.ShapeDtypeStruct((n, D_MODEL), F32),
        compiler_params=pltpu.CompilerParams(
            dimension_semantics=("parallel",), vmem_limit_bytes=_vmem_limit(est)),
        name="outproj_ln",
    )(yc, ya, x2d, w_out, g, b)


def _row_index(n, lanes):
    return lax.broadcasted_iota(jnp.int32, (n, lanes), 0).astype(F32)


def _top16(s, keys, store_val, store_key):
    for r in range(PEER_TOPK):
        m = jnp.max(s, axis=0, keepdims=True)
        key = jnp.min(jnp.where(s == m, keys, jnp.inf), axis=0, keepdims=True)
        s = jnp.where(keys == key, -jnp.inf, s)
        store_val(r, m)
        store_key(r, key)


def _top16_by_row(s, store_val, store_key):
    n, lanes = s.shape
    nslab = n // V7X_SUBLANES
    sublane = _row_index(V7X_SUBLANES, lanes)
    slab_keys = [sublane * float(nslab) + float(a) for a in range(nslab)]
    keys = jnp.concatenate(slab_keys, axis=0)
    for r in range(PEER_TOPK):
        vals = [s[a * V7X_SUBLANES:(a + 1) * V7X_SUBLANES, :] for a in range(nslab)]
        idxs = slab_keys
        while len(vals) > 1:
            nxt_v, nxt_i = [], []
            for a in range(0, len(vals), 2):
                left = vals[a] >= vals[a + 1]
                nxt_v.append(jnp.maximum(vals[a], vals[a + 1]))
                nxt_i.append(jnp.where(left, idxs[a], idxs[a + 1]))
            vals, idxs = nxt_v, nxt_i
        v8, k8 = vals[0], idxs[0]
        for step in (1, 2, 4):
            pv = pltpu.roll(v8, V7X_SUBLANES - step, 0)
            pk = pltpu.roll(k8, V7X_SUBLANES - step, 0)
            left = v8 >= pv
            v8 = jnp.maximum(v8, pv)
            k8 = jnp.where(left, k8, pk)
        m = v8[0:1, :]
        key = k8[0:1, :]
        s = jnp.where(keys == key, -jnp.inf, s)
        store_val(r, m)
        store_key(r, key)
    return m


def _zero_after(x):
    bits = lax.bitcast_convert_type(x, jnp.uint32)
    return jnp.right_shift(jnp.right_shift(bits, jnp.uint32(16)), jnp.uint32(16)).astype(F32)


def _pair_candidates(s1, s2):
    lanes = s1.shape[1]
    row16 = _row_index(PEER_TOPK, lanes)
    row8 = row16[:V7X_SUBLANES]
    sums, keys = [], []
    for a in range(2):
        sums.append(s1[a:a + 1, :] + s2)
        keys.append(row16 + float(a * PEER_TOPK))
    s1_tail = jnp.where(row16 >= 2.0, s1, -jnp.inf)
    sums.append(s1_tail + s2[0:1, :])
    keys.append(row16 * float(PEER_TOPK))
    for b in range(1, 5):
        sums.append(s1_tail[:V7X_SUBLANES] + s2[b:b + 1, :])
        keys.append(row8 * float(PEER_TOPK) + float(b))
    return jnp.concatenate(sums, axis=0), jnp.concatenate(keys, axis=0)


def _route_chunk(sc_ref, lanes, tv_ref, tp_ref, slot, after):
    def store(ref, which):
        def f(r, val):
            ref[slot, which, r:r + 1, :] = val
        return f

    last = _top16_by_row(sc_ref[0, :, lanes], store(tv_ref, 0), store(tp_ref, 0))
    _top16_by_row(sc_ref[1, :, lanes] + _zero_after(last), store(tv_ref, 1), store(tp_ref, 1))
    comb, comb_keys = _pair_candidates(tv_ref[slot, 0], tv_ref[slot, 1])
    if after is not None:
        comb = comb + after
    _top16(comb, comb_keys, store(tv_ref, 2), store(tp_ref, 2))
    top_s = tv_ref[slot, 2]
    pos = tp_ref[slot, 2].astype(jnp.int32)
    a_sel = jnp.right_shift(pos, 4)
    b_sel = jnp.bitwise_and(pos, PEER_TOPK - 1)
    i1 = tp_ref[slot, 0]
    i2 = tp_ref[slot, 1]
    e1 = jnp.zeros_like(top_s)
    e2 = jnp.zeros_like(top_s)
    for a in range(PEER_TOPK):
        e1 = jnp.where(a_sel == a, i1[a:a + 1, :], e1)
        e2 = jnp.where(b_sel == a, i2[a:a + 1, :], e2)
    ex = jnp.exp(top_s - top_s[0:1, :])
    return e1, e2, ex / jnp.sum(ex, axis=0, keepdims=True)


def _subkey_scores(xb_ref, part, tps, wq_ref, keys_ref, out_ref):
    x_part = xb_ref[pl.ds(pl.multiple_of(part * tps, tps), tps), :]
    q = jnp.dot(x_part, wq_ref[...], preferred_element_type=F32)
    for p in range(2):
        out_ref[p] = lax.dot_general(keys_ref[p], q[:, p * PEER_HALF:(p + 1) * PEER_HALF].astype(BF16), NT_DIMS,
                                     preferred_element_type=F32)


def _peer_kernel(x_ref, xn_ref, wq_ref, keys_ref, ut_lo_ref, ut_hi_ref, v_lo_ref, v_hi_ref,
                 lng_ref, lnb_ref, o_ref,
                 xb_ref, xnb_ref, gate_ref, acc_ref, sc_ref, tv_ref, tp_ref, sel1_ref, sel2_ref, selg_ref,
                 tok1_ref, tok2_ref, tokg_ref, *, alpha):
    tm = x_ref.shape[0]
    i = pl.program_id(0)
    j = pl.program_id(1)
    parts = sel1_ref.shape[0]
    tps = sel1_ref.shape[2]

    @pl.when(j == 0)
    def _():
        xb_ref[...] = x_ref[...].astype(BF16)
        xnb_ref[...] = xn_ref[...].astype(BF16)
        acc_ref[...] = jnp.zeros_like(acc_ref)
        _subkey_scores(xnb_ref, 0, tps, wq_ref.at[0], keys_ref, sc_ref.at[0])

        @pl.when(i == 0)
        def _():
            gate_ref[...] = jnp.zeros_like(gate_ref)

        @pl.when(i > 0)
        def _():
            for part in range(parts):
                for c in range(tps // V7X_LANES):
                    lanes = slice(c * V7X_LANES, (c + 1) * V7X_LANES)
                    rows = slice(part * tps + c * V7X_LANES, part * tps + (c + 1) * V7X_LANES)
                    tok1_ref[rows, :] = sel1_ref[part, :, lanes].T
                    tok2_ref[rows, :] = sel2_ref[part, :, lanes].T
                    tokg_ref[rows, :] = selg_ref[part, :, lanes].T
            key_row = lax.broadcasted_iota(jnp.int32, (PEER_NKEYS, PEER_SEL), 0).astype(F32).astype(BF16)
            one = jnp.ones((PEER_NKEYS, PEER_SEL), BF16)
            zero = jnp.zeros((PEER_NKEYS, PEER_SEL), BF16)

            def token_body(t, carry):
                r1 = tok1_ref[pl.ds(t, 1), :].astype(BF16)
                r2 = tok2_ref[pl.ds(t, 1), :].astype(BF16)
                gg = jnp.broadcast_to(tokg_ref[pl.ds(t, 1), :].astype(BF16), (PEER_NKEYS, PEER_SEL))
                hot1 = jnp.where(key_row == r1, one, zero)
                wgt2 = jnp.where(key_row == r2, gg, zero)
                tile = lax.dot_general(hot1, wgt2, NT_DIMS, preferred_element_type=F32)
                packed = pltpu.bitcast(tile.astype(BF16), jnp.uint32)
                gate_ref[pl.ds(pl.multiple_of(t * GATE_PITCH, V7X_SUBLANES), PEER_KEY_HALF), :] = packed
                return carry

            lax.fori_loop(0, tm, token_body, 0, unroll=GATE_UNROLL)

    head = j // parts
    part = j % parts
    sc_cur = sc_ref.at[j % 2]
    base = pl.multiple_of(head * PEER_TOPK, PEER_TOPK)
    after = None
    for c in range(tps // V7X_LANES):
        lanes = slice(c * V7X_LANES, (c + 1) * V7X_LANES)
        e1, e2, gate = _route_chunk(sc_cur, lanes, tv_ref, tp_ref, c, after)
        sel1_ref[part, pl.ds(base, PEER_TOPK), lanes] = e1
        sel2_ref[part, pl.ds(base, PEER_TOPK), lanes] = e2
        selg_ref[part, pl.ds(base, PEER_TOPK), lanes] = gate
        after = _zero_after(e1[0:1, :] + e2[0:1, :] + gate[0:1, :])

    xb = xb_ref[...]
    h_lo = jnp.dot(xb, ut_lo_ref[...], preferred_element_type=F32)
    h_hi = jnp.dot(xb, ut_hi_ref[...], preferred_element_type=F32)
    parts_lo, parts_hi = [], []
    for r in range(PEER_GROUPS):
        word = gate_ref[pl.ds(j * PEER_GROUPS + r, tm, stride=GATE_PITCH), :]
        cols = slice(r * PEER_NKEYS, (r + 1) * PEER_NKEYS)
        gate_lo = lax.bitcast_convert_type(jnp.left_shift(word, jnp.uint32(16)), F32)
        gate_hi = lax.bitcast_convert_type(jnp.bitwise_and(word, jnp.uint32(0xFFFF0000)), F32)
        for h, gate, plist in ((h_lo, gate_lo, parts_lo), (h_hi, gate_hi, parts_hi)):
            hr = h[:, cols]
            gelu2 = hr * (1.0 + lax.erf(hr * math.sqrt(0.5)))
            plist.append((gelu2 * gate).astype(BF16))
    acc = jnp.dot(jnp.concatenate(parts_lo, axis=1), v_lo_ref[...], preferred_element_type=F32)
    acc = acc + jnp.dot(jnp.concatenate(parts_hi, axis=1), v_hi_ref[...], preferred_element_type=F32)
    acc_ref[...] += acc

    nxt = jnp.minimum(j + 1, pl.num_programs(1) - 1)
    _subkey_scores(xnb_ref, nxt % parts, tps, wq_ref.at[nxt // parts], keys_ref, sc_ref.at[(j + 1) % 2])

    @pl.when(j == pl.num_programs(1) - 1)
    def _():
        o_ref[...] = _layer_norm(alpha * x_ref[...] + acc_ref[...], lng_ref[...], lnb_ref[...])


def _peer(x2d, wq, keys, u_t, v_tab, ln_g, ln_b, alpha):
    n = x2d.shape[0]
    tm = min(PEER_ROWS, n)
    nb = n // tm
    assert PEER_STEPS % PEER_HEADS == 0
    parts = PEER_STEPS // PEER_HEADS
    tps = tm // parts
    assert tps % V7X_LANES == 0
    once = pl.Buffered(1)
    vec = pl.BlockSpec((1, D_MODEL), lambda i, j: (0, 0))
    cur = lambda i, j: (jnp.maximum(i - 1, 0), 0)
    gate_bytes = tm * GATE_PITCH * PEER_NKEYS * 4
    est = gate_bytes + 3 * tm * D_MODEL * 4 + 8 * PEER_SLAB * D_MODEL * 2 + tm * D_MODEL * 8 \
        + 8 * tm * PEER_SLAB * 4 + 16 * tm * PEER_SEL * 4
    sel = pltpu.VMEM((parts, PEER_SEL, tps), F32)
    tok = pltpu.VMEM((tm, PEER_SEL), F32)
    return pl.pallas_call(
        functools.partial(_peer_kernel, alpha=alpha),
        grid=(nb + 1, PEER_STEPS),
        in_specs=[pl.BlockSpec((tm, D_MODEL), cur, pipeline_mode=once),
                  pl.BlockSpec((tm, D_MODEL), lambda i, j: (jnp.minimum(i, nb - 1), 0), pipeline_mode=once),
                  pl.BlockSpec((PEER_HEADS, D_MODEL, 2 * PEER_HALF), lambda i, j: (0, 0, 0), pipeline_mode=once),
                  pl.BlockSpec((2, PEER_NKEYS, PEER_HALF), lambda i, j: (0, 0, 0)),
                  pl.BlockSpec((None, D_MODEL, PEER_SLAB), lambda i, j: (j, 0, 0)),
                  pl.BlockSpec((None, D_MODEL, PEER_SLAB), lambda i, j: (j + PEER_STEPS, 0, 0)),
                  pl.BlockSpec((PEER_SLAB, D_MODEL), lambda i, j: (j, 0)),
                  pl.BlockSpec((PEER_SLAB, D_MODEL), lambda i, j: (j + PEER_STEPS, 0)),
                  vec, vec],
        out_specs=pl.BlockSpec((tm, D_MODEL), cur),
        out_shape=jax.ShapeDtypeStruct((n, D_MODEL), F32),
        scratch_shapes=[pltpu.VMEM((tm, D_MODEL), BF16),
                        pltpu.VMEM((tm, D_MODEL), BF16),
                        pltpu.VMEM((tm * GATE_PITCH, PEER_NKEYS), jnp.uint32),
                        pltpu.VMEM((tm, D_MODEL), F32),
                        pltpu.VMEM((2, 2, PEER_NKEYS, tps), F32),
                        pltpu.VMEM((tps // V7X_LANES, 3, PEER_TOPK, V7X_LANES), F32),
                        pltpu.VMEM((tps // V7X_LANES, 3, PEER_TOPK, V7X_LANES), F32),
                        sel, sel, sel, tok, tok, tok],
        compiler_params=pltpu.CompilerParams(
            dimension_semantics=("arbitrary", "arbitrary"), vmem_limit_bytes=_vmem_limit(est)),
        name="peer_route_mix_ln",
    )(x2d, x2d, wq, keys, u_t, u_t, v_tab, v_tab, ln_g, ln_b)


def _rope_tables(seq):
    inv_freq = ROPE_THETA ** (-jnp.arange(0, ATTN_QKDIM, 2, dtype=F32) / ATTN_QKDIM)
    ang = jnp.arange(seq, dtype=F32)[:, None] * inv_freq[None, :]
    ang = jnp.concatenate([ang, ang], axis=-1)
    sign = jnp.where(jnp.arange(ATTN_QKDIM) < ROPE_HALF, -1.0, 1.0).astype(F32)
    reps = ATTN_WIDTH // ATTN_QKDIM
    return jnp.tile(jnp.cos(ang), (1, reps)), jnp.tile(jnp.sin(ang) * sign, (1, reps))


def _trunk(x, layers, depth):
    b, s, _ = x.shape
    alpha = (2 * depth) ** 0.25
    cos, sin = _rope_tables(s)
    x2d = x.reshape(b * s, D_MODEL)
    for l, p in enumerate(layers):
        lam_init = 0.8 - 0.6 * math.exp(-0.3 * l)
        yc, q, k, vt = _inproj(x2d, p["w_in"], p["wv_t"], p["conv_w"], cos, sin, s)
        shp = (b, s, ATTN_WIDTH)
        ya = _attention(q.reshape(shp), k.reshape(shp), vt,
                        p["lq1"], p["lk1"], p["lq2"], p["lk2"], p["subln_g"], lam_init)
        x2d = _outproj(yc, ya.reshape(b * s, ATTN_WIDTH), x2d, p["w_out"], p["ln1_g"], p["ln1_b"], alpha)
        x2d = _peer(x2d, p["wq"], p["keys"], p["u_t"], p["v"], p["ln2_g"], p["ln2_b"], alpha)
    return x2d.reshape(b, s, D_MODEL)


def kernel(x_prompt, x_sample, w_in, w_out, conv_w, lam_q1, lam_k1, lam_q2, lam_k2, subln_g, ln1_g, ln1_b,
           peer_wq, peer_keys, peer_u, peer_v, ln2_g, ln2_b):
    depth = w_in.shape[0]
    v_cols = slice(IN_COLS - ATTN_WIDTH, IN_COLS)

    def pair_order(tab):
        d = tab.shape[1]
        return tab.reshape(PEER_KEY_HALF, 2, PEER_NKEYS, d).transpose(1, 0, 2, 3).reshape(PEER_N, d)

    layers = []
    for l in range(depth):
        row = lambda a: a[l][None, :]
        layers.append(dict(
            w_in=w_in[l].astype(BF16), wv_t=w_in[l][:, v_cols].T.astype(BF16),
            w_out=w_out[l].astype(BF16), conv_w=conv_w[l],
            lq1=row(lam_q1), lk1=row(lam_k1), lq2=row(lam_q2), lk2=row(lam_k2), subln_g=row(subln_g),
            ln1_g=row(ln1_g), ln1_b=row(ln1_b), ln2_g=row(ln2_g), ln2_b=row(ln2_b),
            wq=peer_wq[l].astype(BF16).reshape(D_MODEL, PEER_HEADS, 2 * PEER_HALF).transpose(1, 0, 2),
            keys=peer_keys[l].astype(BF16).reshape(2, V7X_SUBLANES, PEER_NKEYS // V7X_SUBLANES, PEER_HALF)
            .transpose(0, 2, 1, 3).reshape(2, PEER_NKEYS, PEER_HALF),
            u_t=pair_order(peer_u[l].astype(BF16)).reshape(PEER_N // PEER_SLAB, PEER_SLAB, D_MODEL).transpose(0, 2, 1),
            v=pair_order((0.5 * peer_v[l]).astype(BF16))))
    return (_trunk(x_prompt, layers, depth), _trunk(x_sample, layers, depth))
```

```python
import functools
import math

import jax
import jax.numpy as jnp
from jax import lax
from jax.experimental import pallas as pl
from jax.experimental.pallas import tpu as pltpu

F32 = jnp.float32
BF16 = jnp.bfloat16

D_MODEL = 1024
CONV_WIDTH = 512
ATTN_WIDTH = 512
N_ATTN_HEADS = 4
ATTN_VDIM = 128
ATTN_QKDIM = 64
ROPE_HALF = ATTN_QKDIM // 2
IN_COLS = 3 * CONV_WIDTH + 3 * ATTN_WIDTH
ROPE_THETA = 10000.0
PEER_HEADS = 8
PEER_NKEYS = 128
PEER_N = PEER_NKEYS * PEER_NKEYS
PEER_HALF = 128
PEER_QCOLS = PEER_HEADS * 2 * PEER_HALF
PEER_TOPK = 16
PEER_SEL = PEER_HEADS * PEER_TOPK
LN_EPS = 1e-5

V7X_LANES = 128
V7X_SUBLANES = 8
V7X_VMEM_BYTES = 64 * 1024 * 1024
MIB = 1024 * 1024

INPROJ_ROWS = 512
ATTN_Q_ROWS = 256
ATTN_KEY_CHUNK = 512
ATTN_LOOKAHEAD = 6
OUTPROJ_ROWS = 512
PEER_ROWS = 512
PEER_SLAB = 512
PEER_GROUPS = PEER_SLAB // PEER_NKEYS
PEER_KEY_HALF = PEER_NKEYS // 2
PEER_STEPS = PEER_KEY_HALF // PEER_GROUPS
GATE_PITCH = PEER_KEY_HALF + V7X_SUBLANES
GATE_UNROLL = 128

NT_DIMS = (((1,), (1,)), ((), ()))


def _vmem_limit(nbytes):
    return int(min(V7X_VMEM_BYTES - 6 * MIB, max(32 * MIB, nbytes)))


def _layer_norm(z, g, b):
    mu = jnp.mean(z, axis=-1, keepdims=True)
    zc = z - mu
    var = jnp.mean(zc * zc, axis=-1, keepdims=True)
    return zc * lax.rsqrt(var + LN_EPS) * g + b


def _inproj_kernel(x_ref, xp_ref, xn_ref, w_ref, wvt_ref, cw_ref, cos_ref, sin_ref,
                   yc_ref, q_ref, k_ref, vt_ref, *, blocks_per_seq):
    tm = x_ref.shape[0]
    pos = lax.rem(pl.program_id(0), blocks_per_seq)
    xe = jnp.concatenate([x_ref[...], xp_ref[...], xn_ref[...]], axis=0).astype(BF16)
    xb = xe[:tm]

    def proj(v, c0, c1):
        return jnp.dot(v, w_ref[:, c0:c1], preferred_element_type=F32)

    g_b = proj(xb, 0, CONV_WIDTH)
    ue = proj(xe, CONV_WIDTH, 2 * CONV_WIDTH) * proj(xe, 2 * CONV_WIDTH, 3 * CONV_WIDTH)
    u = ue[:tm]
    u_prev = jnp.where(pos == 0, 0.0, ue[tm + V7X_SUBLANES - 1:tm + V7X_SUBLANES])
    u_next = jnp.where(pos == blocks_per_seq - 1, 0.0, ue[tm + V7X_SUBLANES:tm + V7X_SUBLANES + 1])
    row = lax.broadcasted_iota(jnp.int32, u.shape, 0)
    u_m1 = jnp.where(row == 0, u_prev, pltpu.roll(u, 1, 0))
    u_p1 = jnp.where(row == tm - 1, u_next, pltpu.roll(u, tm - 1, 0))
    conv = u_m1 * cw_ref[0:1, :] + u * cw_ref[1:2, :] + u_p1 * cw_ref[2:3, :]
    yc_ref[...] = (g_b * conv).astype(yc_ref.dtype)

    cos = cos_ref[...]
    sin = sin_ref[...]
    lane = lax.broadcasted_iota(jnp.int32, cos.shape, 1)
    first_half = jnp.bitwise_and(lane, ATTN_QKDIM - 1) < ROPE_HALF

    def rope(t):
        n = t.shape[1]
        rot = jnp.where(first_half, pltpu.roll(t, n - ROPE_HALF, 1), pltpu.roll(t, ROPE_HALF, 1))
        return t * cos + rot * sin

    c = 3 * CONV_WIDTH
    scale = math.log2(math.e) / math.sqrt(ATTN_QKDIM)
    q_ref[...] = (rope(proj(xb, c, c + ATTN_WIDTH)) * scale).astype(q_ref.dtype)
    k_ref[...] = rope(proj(xb, c + ATTN_WIDTH, c + 2 * ATTN_WIDTH)).astype(k_ref.dtype)
    vt_ref[...] = lax.dot_general(wvt_ref[...], xb, NT_DIMS, preferred_element_type=F32).astype(vt_ref.dtype)


def _inproj(x2d, w_in, wv_t, conv_w, cos, sin, seq):
    n = x2d.shape[0]
    tm = min(INPROJ_ROWS, seq)
    bps = seq // tm
    nblk = n // tm
    r8 = tm // V7X_SUBLANES
    last8 = n // V7X_SUBLANES - 1
    row_spec = lambda w: pl.BlockSpec((tm, w), lambda i: (i, 0))
    tab_spec = pl.BlockSpec((tm, ATTN_WIDTH), lambda i: (lax.rem(i, bps), 0))
    out = jax.ShapeDtypeStruct((n, CONV_WIDTH), BF16)
    est = 2 * (tm * D_MODEL * 4 + D_MODEL * IN_COLS * 2 + 2 * tm * ATTN_WIDTH * 4 + 4 * tm * 512 * 2) \
        + 12 * tm * 512 * 4 + 2 * tm * D_MODEL * 4
    return pl.pallas_call(
        functools.partial(_inproj_kernel, blocks_per_seq=bps),
        grid=(nblk,),
        in_specs=[
            row_spec(D_MODEL),
            pl.BlockSpec((V7X_SUBLANES, D_MODEL), lambda i: (jnp.maximum(i * r8 - 1, 0), 0)),
            pl.BlockSpec((V7X_SUBLANES, D_MODEL), lambda i: (jnp.minimum((i + 1) * r8, last8), 0)),
            pl.BlockSpec((D_MODEL, IN_COLS), lambda i: (0, 0)),
            pl.BlockSpec((ATTN_WIDTH, D_MODEL), lambda i: (0, 0)),
            pl.BlockSpec((3, CONV_WIDTH), lambda i: (0, 0)),
            tab_spec, tab_spec,
        ],
        out_specs=[row_spec(CONV_WIDTH)] * 3 + [pl.BlockSpec((ATTN_WIDTH, tm), lambda i: (0, i))],
        out_shape=[out] * 3 + [jax.ShapeDtypeStruct((ATTN_WIDTH, n), BF16)],
        compiler_params=pltpu.CompilerParams(
            dimension_semantics=("parallel",), vmem_limit_bytes=_vmem_limit(est)),
        name="inproj_conv_rope",
    )(x2d, x2d, x2d, w_in, wv_t, conv_w, cos, sin)


def _attn_kernel(q_ref, k_ref, vt_ref, lq1_ref, lk1_ref, lq2_ref, lk2_ref, g_ref, o_ref, *, lam_init):
    q = q_ref[...]
    s = k_ref.shape[0]
    ck = min(ATTN_KEY_CHUNK, s)
    lane = lax.broadcasted_iota(jnp.int32, q.shape, 1)
    zero = jnp.zeros_like(q)
    q_maps = (jnp.where(lane < ATTN_QKDIM, q, zero), jnp.where(lane >= ATTN_QKDIM, q, zero))

    m_run, l_run, o_run = [None, None], [None, None], [None, None]
    items = [(c, mp) for c in range(s // ck) for mp in range(2)]

    def scores(item):
        c, mp = item
        return lax.dot_general(k_ref[c * ck:(c + 1) * ck, :], q_maps[mp], NT_DIMS,
                               preferred_element_type=F32)

    pending = [scores(it) for it in items[:ATTN_LOOKAHEAD]]
    for n, (c, mp) in enumerate(items):
        keys = slice(c * ck, (c + 1) * ck)
        if n + ATTN_LOOKAHEAD < len(items):
            pending.append(scores(items[n + ATTN_LOOKAHEAD]))
        st = pending.pop(0)
        m_c = jnp.max(st, axis=0, keepdims=True)
        m_new = m_c if c == 0 else jnp.maximum(m_run[mp], m_c)
        e = jnp.exp2(st - m_new)
        l_c = jnp.sum(e, axis=0, keepdims=True)
        pv = jnp.dot(vt_ref[:, keys], e.astype(vt_ref.dtype), preferred_element_type=F32)
        if c == 0:
            l_run[mp], o_run[mp] = l_c, pv
        else:
            rescale = jnp.exp2(m_run[mp] - m_new)
            l_run[mp] = l_run[mp] * rescale + l_c
            o_run[mp] = o_run[mp] * rescale + pv
        m_run[mp] = m_new

    lam = (jnp.exp(jnp.sum(lq1_ref[...] * lk1_ref[...], axis=-1, keepdims=True))
           - jnp.exp(jnp.sum(lq2_ref[...] * lk2_ref[...], axis=-1, keepdims=True)) + lam_init)
    ot = o_run[0] / l_run[0] - lam * (o_run[1] / l_run[1])
    ot = ot * lax.rsqrt(jnp.mean(ot * ot, axis=0, keepdims=True) + LN_EPS)
    o_ref[...] = (ot.T * g_ref[...] * (1.0 - lam_init)).astype(o_ref.dtype)


def _attention(q, k, vt, lq1, lk1, lq2, lk2, subln_g, lam_init):
    b, s, _ = q.shape
    tq = min(ATTN_Q_ROWS, s)
    q_spec = pl.BlockSpec((None, tq, ATTN_VDIM), lambda bi, h, i: (bi, i, h))
    k_spec = pl.BlockSpec((None, s, ATTN_VDIM), lambda bi, h, i: (bi, 0, h))
    vt_spec = pl.BlockSpec((ATTN_VDIM, s), lambda bi, h, i: (h, bi))
    vec = lambda w: pl.BlockSpec((1, w), lambda bi, h, i: (0, 0))
    est = 4 * s * ATTN_VDIM * 2 + 2 * tq * s * (4 + 4 + 2) + 8 * tq * ATTN_VDIM * 4
    return pl.pallas_call(
        functools.partial(_attn_kernel, lam_init=lam_init),
        grid=(b, N_ATTN_HEADS, s // tq),
        in_specs=[q_spec, k_spec, vt_spec, vec(ATTN_QKDIM), vec(ATTN_QKDIM), vec(ATTN_QKDIM),
                  vec(ATTN_QKDIM), vec(ATTN_VDIM)],
        out_specs=q_spec,
        out_shape=jax.ShapeDtypeStruct((b, s, ATTN_WIDTH), BF16),
        compiler_params=pltpu.CompilerParams(
            dimension_semantics=("parallel", "parallel", "parallel"), vmem_limit_bytes=_vmem_limit(est)),
        name="diff_attention",
    )(q, k, vt, lq1, lk1, lq2, lk2, subln_g)


def _outproj_kernel(yc_ref, ya_ref, x_ref, w_ref, g_ref, b_ref, o_ref, *, alpha):
    half = x_ref.shape[0] // 2
    mixed = []
    for r in range(2):
        rows = slice(r * half, (r + 1) * half)
        m = jnp.dot(yc_ref[rows, :], w_ref[:CONV_WIDTH, :], preferred_element_type=F32)
        mixed.append(m + jnp.dot(ya_ref[rows, :], w_ref[CONV_WIDTH:, :], preferred_element_type=F32))
    for r in range(2):
        rows = slice(r * half, (r + 1) * half)
        o_ref[rows, :] = _layer_norm(alpha * x_ref[rows, :] + mixed[r], g_ref[...], b_ref[...])


def _outproj(yc, ya, x2d, w_out, g, b, alpha):
    n = x2d.shape[0]
    tm = min(OUTPROJ_ROWS, n)
    row_spec = lambda w: pl.BlockSpec((tm, w), lambda i: (i, 0))
    vec = pl.BlockSpec((1, D_MODEL), lambda i: (0, 0))
    est = 2 * (2 * tm * 512 * 2 + 2 * tm * D_MODEL * 4 + D_MODEL * D_MODEL * 2) + 4 * tm * D_MODEL * 4
    return pl.pallas_call(
        functools.partial(_outproj_kernel, alpha=alpha),
        grid=(n // tm,),
        in_specs=[row_spec(CONV_WIDTH), row_spec(ATTN_WIDTH), row_spec(D_MODEL),
                  pl.BlockSpec((D_MODEL, D_MODEL), lambda i: (0, 0)), vec, vec],
        out_specs=row_spec(D_MODEL),
        out_shape=jax.ShapeDtypeStruct((n, D_MODEL), F32),
        compiler_params=pltpu.CompilerParams(
            dimension_semantics=("parallel",), vmem_limit_bytes=_vmem_limit(est)),
        name="outproj_ln",
    )(yc, ya, x2d, w_out, g, b)


def _row_index(n, lanes):
    return lax.broadcasted_iota(jnp.int32, (n, lanes), 0).astype(F32)


def _top16(s, keys, store_val, store_key):
    for r in range(PEER_TOPK):
        m = jnp.max(s, axis=0, keepdims=True)
        key = jnp.min(jnp.where(s == m, keys, jnp.inf), axis=0, keepdims=True)
        s = jnp.where(keys == key, -jnp.inf, s)
        store_val(r, m)
        store_key(r, key)


def _top16_by_row(s, store_val, store_key):
    n, lanes = s.shape
    nslab = n // V7X_SUBLANES
    sublane = _row_index(V7X_SUBLANES, lanes)
    slab_keys = [sublane * float(nslab) + float(a) for a in range(nslab)]
    keys = jnp.concatenate(slab_keys, axis=0)
    for r in range(PEER_TOPK):
        vals = [s[a * V7X_SUBLANES:(a + 1) * V7X_SUBLANES, :] for a in range(nslab)]
        idxs = slab_keys
        while len(vals) > 1:
            nxt_v, nxt_i = [], []
            for a in range(0, len(vals), 2):
                left = vals[a] >= vals[a + 1]
                nxt_v.append(jnp.maximum(vals[a], vals[a + 1]))
                nxt_i.append(jnp.where(left, idxs[a], idxs[a + 1]))
            vals, idxs = nxt_v, nxt_i
        v8, k8 = vals[0], idxs[0]
        for step in (1, 2, 4):
            pv = pltpu.roll(v8, V7X_SUBLANES - step, 0)
            pk = pltpu.roll(k8, V7X_SUBLANES - step, 0)
            left = v8 >= pv
            v8 = jnp.maximum(v8, pv)
            k8 = jnp.where(left, k8, pk)
        m = v8[0:1, :]
        key = k8[0:1, :]
        s = jnp.where(keys == key, -jnp.inf, s)
        store_val(r, m)
        store_key(r, key)
    return m


def _zero_after(x):
    bits = lax.bitcast_convert_type(x, jnp.uint32)
    return jnp.right_shift(jnp.right_shift(bits, jnp.uint32(16)), jnp.uint32(16)).astype(F32)


def _pair_candidates(s1, s2):
    lanes = s1.shape[1]
    row16 = _row_index(PEER_TOPK, lanes)
    row8 = row16[:V7X_SUBLANES]
    sums, keys = [], []
    for a in range(2):
        sums.append(s1[a:a + 1, :] + s2)
        keys.append(row16 + float(a * PEER_TOPK))
    s1_tail = jnp.where(row16 >= 2.0, s1, -jnp.inf)
    sums.append(s1_tail + s2[0:1, :])
    keys.append(row16 * float(PEER_TOPK))
    for b in range(1, 5):
        sums.append(s1_tail[:V7X_SUBLANES] + s2[b:b + 1, :])
        keys.append(row8 * float(PEER_TOPK) + float(b))
    return jnp.concatenate(sums, axis=0), jnp.concatenate(keys, axis=0)


def _route_chunk(sc_ref, lanes, tv_ref, tp_ref, slot, after):
    def store(ref, which):
        def f(r, val):
            ref[slot, which, r:r + 1, :] = val
        return f

    last = _top16_by_row(sc_ref[0, :, lanes], store(tv_ref, 0), store(tp_ref, 0))
    _top16_by_row(sc_ref[1, :, lanes] + _zero_after(last), store(tv_ref, 1), store(tp_ref, 1))
    comb, comb_keys = _pair_candidates(tv_ref[slot, 0], tv_ref[slot, 1])
    if after is not None:
        comb = comb + after
    _top16(comb, comb_keys, store(tv_ref, 2), store(tp_ref, 2))
    top_s = tv_ref[slot, 2]
    pos = tp_ref[slot, 2].astype(jnp.int32)
    a_sel = jnp.right_shift(pos, 4)
    b_sel = jnp.bitwise_and(pos, PEER_TOPK - 1)
    i1 = tp_ref[slot, 0]
    i2 = tp_ref[slot, 1]
    e1 = jnp.zeros_like(top_s)
    e2 = jnp.zeros_like(top_s)
    for a in range(PEER_TOPK):
        e1 = jnp.where(a_sel == a, i1[a:a + 1, :], e1)
        e2 = jnp.where(b_sel == a, i2[a:a + 1, :], e2)
    ex = jnp.exp(top_s - top_s[0:1, :])
    return e1, e2, ex / jnp.sum(ex, axis=0, keepdims=True)


def _subkey_scores(xb_ref, part, tps, wq_ref, keys_ref, out_ref):
    x_part = xb_ref[pl.ds(pl.multiple_of(part * tps, tps), tps), :]
    q = jnp.dot(x_part, wq_ref[...], preferred_element_type=F32)
    for p in range(2):
        out_ref[p] = lax.dot_general(keys_ref[p], q[:, p * PEER_HALF:(p + 1) * PEER_HALF].astype(BF16), NT_DIMS,
                                     preferred_element_type=F32)


def _peer_kernel(x_ref, xn_ref, wq_ref, keys_ref, ut_lo_ref, ut_hi_ref, v_lo_ref, v_hi_ref,
                 lng_ref, lnb_ref, o_ref,
                 xb_ref, xnb_ref, gate_ref, acc_ref, sc_ref, tv_ref, tp_ref, sel1_ref, sel2_ref, selg_ref,
                 tok1_ref, tok2_ref, tokg_ref, *, alpha):
    tm = x_ref.shape[0]
    i = pl.program_id(0)
    j = pl.program_id(1)
    parts = sel1_ref.shape[0]
    tps = sel1_ref.shape[2]

    @pl.when(j == 0)
    def _():
        xb_ref[...] = x_ref[...].astype(BF16)
        xnb_ref[...] = xn_ref[...].astype(BF16)
        acc_ref[...] = jnp.zeros_like(acc_ref)
        _subkey_scores(xnb_ref, 0, tps, wq_ref.at[0], keys_ref, sc_ref.at[0])

        @pl.when(i == 0)
        def _():
            gate_ref[...] = jnp.zeros_like(gate_ref)

        @pl.when(i > 0)
        def _():
            for part in range(parts):
                for c in range(tps // V7X_LANES):
                    lanes = slice(c * V7X_LANES, (c + 1) * V7X_LANES)
                    rows = slice(part * tps + c * V7X_LANES, part * tps + (c + 1) * V7X_LANES)
                    tok1_ref[rows, :] = sel1_ref[part, :, lanes].T
                    tok2_ref[rows, :] = sel2_ref[part, :, lanes].T
                    tokg_ref[rows, :] = selg_ref[part, :, lanes].T
            key_row = lax.broadcasted_iota(jnp.int32, (PEER_NKEYS, PEER_SEL), 0).astype(F32).astype(BF16)
            one = jnp.ones((PEER_NKEYS, PEER_SEL), BF16)
            zero = jnp.zeros((PEER_NKEYS, PEER_SEL), BF16)

            def token_body(t, carry):
                r1 = tok1_ref[pl.ds(t, 1), :].astype(BF16)
                r2 = tok2_ref[pl.ds(t, 1), :].astype(BF16)
                gg = jnp.broadcast_to(tokg_ref[pl.ds(t, 1), :].astype(BF16), (PEER_NKEYS, PEER_SEL))
                hot1 = jnp.where(key_row == r1, one, zero)
                wgt2 = jnp.where(key_row == r2, gg, zero)
                tile = lax.dot_general(hot1, wgt2, NT_DIMS, preferred_element_type=F32)
                packed = pltpu.bitcast(tile.astype(BF16), jnp.uint32)
                gate_ref[pl.ds(pl.multiple_of(t * GATE_PITCH, V7X_SUBLANES), PEER_KEY_HALF), :] = packed
                return carry

            lax.fori_loop(0, tm, token_body, 0, unroll=GATE_UNROLL)

    head = j // parts
    part = j % parts
    sc_cur = sc_ref.at[j % 2]
    base = pl.multiple_of(head * PEER_TOPK, PEER_TOPK)
    after = None
    for c in range(tps // V7X_LANES):
        lanes = slice(c * V7X_LANES, (c + 1) * V7X_LANES)
        e1, e2, gate = _route_chunk(sc_cur, lanes, tv_ref, tp_ref, c, after)
        sel1_ref[part, pl.ds(base, PEER_TOPK), lanes] = e1
        sel2_ref[part, pl.ds(base, PEER_TOPK), lanes] = e2
        selg_ref[part, pl.ds(base, PEER_TOPK), lanes] = gate
        after = _zero_after(e1[0:1, :] + e2[0:1, :] + gate[0:1, :])

    xb = xb_ref[...]
    h_lo = jnp.dot(xb, ut_lo_ref[...], preferred_element_type=F32)
    h_hi = jnp.dot(xb, ut_hi_ref[...], preferred_element_type=F32)
    parts_lo, parts_hi = [], []
    for r in range(PEER_GROUPS):
        word = gate_ref[pl.ds(j * PEER_GROUPS + r, tm, stride=GATE_PITCH), :]
        cols = slice(r * PEER_NKEYS, (r + 1) * PEER_NKEYS)
        gate_lo = lax.bitcast_convert_type(jnp.left_shift(word, jnp.uint32(16)), F32)
        gate_hi = lax.bitcast_convert_type(jnp.bitwise_and(word, jnp.uint32(0xFFFF0000)), F32)
        for h, gate, plist in ((h_lo, gate_lo, parts_lo), (h_hi, gate_hi, parts_hi)):
            hr = h[:, cols]
            gelu2 = hr * (1.0 + lax.erf(hr * math.sqrt(0.5)))
            plist.append((gelu2 * gate).astype(BF16))
    acc = jnp.dot(jnp.concatenate(parts_lo, axis=1), v_lo_ref[...], preferred_element_type=F32)
    acc = acc + jnp.dot(jnp.concatenate(parts_hi, axis=1), v_hi_ref[...], preferred_element_type=F32)
    acc_ref[...] += acc

    nxt = jnp.minimum(j + 1, pl.num_programs(1) - 1)
    _subkey_scores(xnb_ref, nxt % parts, tps, wq_ref.at[nxt // parts], keys_ref, sc_ref.at[(j + 1) % 2])

    @pl.when(j == pl.num_programs(1) - 1)
    def _():
        o_ref[...] = _layer_norm(alpha * x_ref[...] + acc_ref[...], lng_ref[...], lnb_ref[...])


def _peer(x2d, wq, keys, u_t, v_tab, ln_g, ln_b, alpha):
    n = x2d.shape[0]
    tm = min(PEER_ROWS, n)
    nb = n // tm
    assert PEER_STEPS % PEER_HEADS == 0
    parts = PEER_STEPS // PEER_HEADS
    tps = tm // parts
    assert tps % V7X_LANES == 0
    once = pl.Buffered(1)
    vec = pl.BlockSpec((1, D_MODEL), lambda i, j: (0, 0))
    cur = lambda i, j: (jnp.maximum(i - 1, 0), 0)
    gate_bytes = tm * GATE_PITCH * PEER_NKEYS * 4
    est = gate_bytes + 3 * tm * D_MODEL * 4 + 8 * PEER_SLAB * D_MODEL * 2 + tm * D_MODEL * 8 \
        + 8 * tm * PEER_SLAB * 4 + 16 * tm * PEER_SEL * 4
    sel = pltpu.VMEM((parts, PEER_SEL, tps), F32)
    tok = pltpu.VMEM((tm, PEER_SEL), F32)
    return pl.pallas_call(
        functools.partial(_peer_kernel, alpha=alpha),
        grid=(nb + 1, PEER_STEPS),
        in_specs=[pl.BlockSpec((tm, D_MODEL), cur, pipeline_mode=once),
                  pl.BlockSpec((tm, D_MODEL), lambda i, j: (jnp.minimum(i, nb - 1), 0), pipeline_mode=once),
                  pl.BlockSpec((PEER_HEADS, D_MODEL, 2 * PEER_HALF), lambda i, j: (0, 0, 0), pipeline_mode=once),
                  pl.BlockSpec((2, PEER_NKEYS, PEER_HALF), lambda i, j: (0, 0, 0)),
                  pl.BlockSpec((None, D_MODEL, PEER_SLAB), lambda i, j: (j, 0, 0)),
                  pl.BlockSpec((None, D_MODEL, PEER_SLAB), lambda i, j: (j + PEER_STEPS, 0, 0)),
                  pl.BlockSpec((PEER_SLAB, D_MODEL), lambda i, j: (j, 0)),
                  pl.BlockSpec((PEER_SLAB, D_MODEL), lambda i, j: (j + PEER_STEPS, 0)),
                  vec, vec],
        out_specs=pl.BlockSpec((tm, D_MODEL), cur),
        out_shape=jax.ShapeDtypeStruct((n, D_MODEL), F32),
        scratch_shapes=[pltpu.VMEM((tm, D_MODEL), BF16),
                        pltpu.VMEM((tm, D_MODEL), BF16),
                        pltpu.VMEM((tm * GATE_PITCH, PEER_NKEYS), jnp.uint32),
                        pltpu.VMEM((tm, D_MODEL), F32),
                        pltpu.VMEM((2, 2, PEER_NKEYS, tps), F32),
                        pltpu.VMEM((tps // V7X_LANES, 3, PEER_TOPK, V7X_LANES), F32),
                        pltpu.VMEM((tps // V7X_LANES, 3, PEER_TOPK, V7X_LANES), F32),
                        sel, sel, sel, tok, tok, tok],
        compiler_params=pltpu.CompilerParams(
            dimension_semantics=("arbitrary", "arbitrary"), vmem_limit_bytes=_vmem_limit(est)),
        name="peer_route_mix_ln",
    )(x2d, x2d, wq, keys, u_t, u_t, v_tab, v_tab, ln_g, ln_b)


def _rope_tables(seq):
    inv_freq = ROPE_THETA ** (-jnp.arange(0, ATTN_QKDIM, 2, dtype=F32) / ATTN_QKDIM)
    ang = jnp.arange(seq, dtype=F32)[:, None] * inv_freq[None, :]
    ang = jnp.concatenate([ang, ang], axis=-1)
    sign = jnp.where(jnp.arange(ATTN_QKDIM) < ROPE_HALF, -1.0, 1.0).astype(F32)
    reps = ATTN_WIDTH // ATTN_QKDIM
    return jnp.tile(jnp.cos(ang), (1, reps)), jnp.tile(jnp.sin(ang) * sign, (1, reps))


def _trunk(x, layers, depth):
    b, s, _ = x.shape
    alpha = (2 * depth) ** 0.25
    cos, sin = _rope_tables(s)
    x2d = x.reshape(b * s, D_MODEL)
    for l, p in enumerate(layers):
        lam_init = 0.8 - 0.6 * math.exp(-0.3 * l)
        yc, q, k, vt = _inproj(x2d, p["w_in"], p["wv_t"], p["conv_w"], cos, sin, s)
        shp = (b, s, ATTN_WIDTH)
        ya = _attention(q.reshape(shp), k.reshape(shp), vt,
                        p["lq1"], p["lk1"], p["lq2"], p["lk2"], p["subln_g"], lam_init)
        x2d = _outproj(yc, ya.reshape(b * s, ATTN_WIDTH), x2d, p["w_out"], p["ln1_g"], p["ln1_b"], alpha)
        x2d = _peer(x2d, p["wq"], p["keys"], p["u_t"], p["v"], p["ln2_g"], p["ln2_b"], alpha)
    return x2d.reshape(b, s, D_MODEL)


def kernel(x_prompt, x_sample, w_in, w_out, conv_w, lam_q1, lam_k1, lam_q2, lam_k2, subln_g, ln1_g, ln1_b,
           peer_wq, peer_keys, peer_u, peer_v, ln2_g, ln2_b):
    depth = w_in.shape[0]
    v_cols = slice(IN_COLS - ATTN_WIDTH, IN_COLS)

    def pair_order(tab):
        d = tab.shape[1]
        return tab.reshape(PEER_KEY_HALF, 2, PEER_NKEYS, d).transpose(1, 0, 2, 3).reshape(PEER_N, d)

    layers = []
    for l in range(depth):
        row = lambda a: a[l][None, :]
        layers.append(dict(
            w_in=w_in[l].astype(BF16), wv_t=w_in[l][:, v_cols].T.astype(BF16),
            w_out=w_out[l].astype(BF16), conv_w=conv_w[l],
            lq1=row(lam_q1), lk1=row(lam_k1), lq2=row(lam_q2), lk2=row(lam_k2), subln_g=row(subln_g),
            ln1_g=row(ln1_g), ln1_b=row(ln1_b), ln2_g=row(ln2_g), ln2_b=row(ln2_b),
            wq=peer_wq[l].astype(BF16).reshape(D_MODEL, PEER_HEADS, 2 * PEER_HALF).transpose(1, 0, 2),
            keys=peer_keys[l].astype(BF16).reshape(2, V7X_SUBLANES, PEER_NKEYS // V7X_SUBLANES, PEER_HALF)
            .transpose(0, 2, 1, 3).reshape(2, PEER_NKEYS, PEER_HALF),
            u_t=pair_order(peer_u[l].astype(BF16)).reshape(PEER_N // PEER_SLAB, PEER_SLAB, D_MODEL).transpose(0, 2, 1),
            v=pair_order((0.5 * peer_v[l]).astype(BF16))))
    return (_trunk(x_prompt, layers, depth), _trunk(x_sample, layers, depth))
```
